```python
import jax
import jax.numpy as jnp
from jax import lax
import numpy as np

D_MODEL = 2048
BATCH = 1
SEQ = 8192
DEPTH = 2
DEC_BATCH = 128
DEC_SEQ = 1
PAST_LEN = 2048
PAGE_SIZE = 128

N_A_LAYERS = DEPTH // 2
N_B_LAYERS = DEPTH - N_A_LAYERS
RET_HEADS = 8
RET_KEY_DIM = D_MODEL // RET_HEADS
RET_VAL_DIM = 2 * RET_KEY_DIM
RET_CHUNK = 128
RET_QK_W = RET_HEADS * RET_KEY_DIM
RET_V_W = RET_HEADS * RET_VAL_DIM
RET_IN_W = 2 * RET_QK_W + 2 * RET_V_W
ATT_HEADS = 16
HEAD_DIM = D_MODEL // ATT_HEADS
ATT_W = ATT_HEADS * HEAD_DIM
MOBA_BLOCK = 256
MOBA_TOPK = 3
MOBA_QBLOCK = 32
FFN_DIM = ((8 * D_MODEL // 3 + 255) // 256) * 256
ROPE_THETA = 10000.0
NORM_EPS = 1e-6
GN_EPS = 1e-5
NEG_INF = -1e30
N_PAGES = PAST_LEN // PAGE_SIZE
N_POOL_PAGES = DEC_BATCH * N_PAGES + max(1, (DEC_BATCH * N_PAGES) // 4)

kernel_name = "yoco_retnet_moba_macaron_step"


def rms_norm(x, g):
    x32 = x.astype(jnp.float32)
    y = x32 * lax.rsqrt(jnp.mean(x32 * x32, axis=-1, keepdims=True) + NORM_EPS)
    return (y * g.astype(jnp.float32)).astype(x.dtype)


def swiglu(x, w_in, w_out):
    gate, up = jnp.split(x @ w_in, 2, axis=-1)
    return (jax.nn.silu(gate) * up) @ w_out


def rope(x, pos):
    d = x.shape[-1]
    inv = ROPE_THETA ** (-jnp.arange(0, d, 2, dtype=jnp.float32) / d)
    ang = pos.astype(jnp.float32)[:, None] * inv[None, :]
    cos = jnp.cos(ang)[None, :, None, :]
    sin = jnp.sin(ang)[None, :, None, :]
    x32 = x.astype(jnp.float32)
    x1, x2 = x32[..., : d // 2], x32[..., d // 2:]
    return jnp.concatenate([x1 * cos - x2 * sin, x2 * cos + x1 * sin], axis=-1).astype(x.dtype)


def retention_log_decay():
    return jnp.log(1.0 - 2.0 ** (-5.0 - jnp.arange(RET_HEADS, dtype=jnp.float32)))


def retention_chunk(q, k, v, s, log_g):
    L = q.shape[1]
    i = jnp.arange(L, dtype=jnp.float32)
    diff = i[:, None] - i[None, :]
    decay = jnp.where(diff >= 0, jnp.exp(log_g[:, None, None] * jnp.maximum(diff, 0.0)), 0.0)
    qf, kf, vf = q.astype(jnp.float32), k.astype(jnp.float32), v.astype(jnp.float32)
    a = jnp.einsum('bihd,bjhd->bhij', qf, kf) * decay[None]
    inner = jnp.einsum('bhij,bjhe->bihe', a, vf)
    cross = jnp.einsum('bihd,bhde->bihe', qf, s) * jnp.exp(log_g[None, :] * (i[:, None] + 1.0))[None, :, :, None]
    kdec = kf * jnp.exp(log_g[None, :] * (L - 1.0 - i)[:, None])[None, :, :, None]
    s_new = jnp.exp(log_g * L)[None, :, None, None] * s + jnp.einsum('bjhd,bjhe->bhde', kdec, vf)
    return inner + cross, s_new


def retention_prompt(q, k, v, log_g):
    b, s = q.shape[:2]
    nc = s // RET_CHUNK

    def to_chunks(a):
        return a.reshape(b, nc, RET_CHUNK, *a.shape[2:]).swapaxes(0, 1)

    def step(state, xs):
        o, state = retention_chunk(xs[0], xs[1], xs[2], state, log_g)
        return state, o

    s0 = jnp.zeros((b, RET_HEADS, RET_KEY_DIM, RET_VAL_DIM), jnp.float32)
    s_fin, o = lax.scan(step, s0, (to_chunks(q), to_chunks(k), to_chunks(v)))
    return o.swapaxes(0, 1).reshape(b, s, RET_HEADS, RET_VAL_DIM), s_fin


def head_group_norm(o):
    o32 = o.astype(jnp.float32)
    mu = jnp.mean(o32, axis=-1, keepdims=True)
    var = jnp.mean(jnp.square(o32 - mu), axis=-1, keepdims=True)
    return (o32 - mu) * lax.rsqrt(var + GN_EPS)


def moba_prompt(q, k, v):
    b, s, nh, hd = q.shape
    nb = -(-s // MOBA_BLOCK)
    pad = nb * MOBA_BLOCK - s
    kh = jnp.pad(k, ((0, 0), (0, pad), (0, 0), (0, 0))).transpose(0, 2, 1, 3)
    vh = jnp.pad(v, ((0, 0), (0, pad), (0, 0), (0, 0))).transpose(0, 2, 1, 3)
    kblk = kh.reshape(b, nh, nb, MOBA_BLOCK, hd)
    vblk = vh.reshape(b, nh, nb, MOBA_BLOCK, hd)
    kmean = jnp.mean(kblk, axis=3, dtype=jnp.float32)
    n_sel = min(MOBA_TOPK, nb)
    nq = s // MOBA_QBLOCK
    qc = q.reshape(b, nq, MOBA_QBLOCK, nh, hd).transpose(1, 0, 3, 2, 4)
    bi = jnp.arange(b)[:, None, None, None]
    hi = jnp.arange(nh)[None, :, None, None]
    scale = hd ** -0.5

    def attend_block(args):
        qb, c = args
        t = c * MOBA_QBLOCK + jnp.arange(MOBA_QBLOCK)
        cb = (c * MOBA_QBLOCK) // MOBA_BLOCK
        s_blk = jnp.einsum('bhqd,bhnd->bhqn', qb.astype(jnp.float32), kmean)
        s_blk = jnp.where(jnp.arange(nb) < cb, s_blk, NEG_INF)
        _, idx = lax.top_k(s_blk, n_sel)
        valid = jnp.arange(n_sel) < cb
        kg = kblk[bi, hi, idx]
        vg = vblk[bi, hi, idx]
        s_sel = jnp.einsum('bhqd,bhqjpd->bhqjp', qb, kg).astype(jnp.float32) * scale
        s_sel = jnp.where(valid[:, None], s_sel, NEG_INF).reshape(b, nh, MOBA_QBLOCK, n_sel * MOBA_BLOCK)
        k_own = lax.dynamic_slice_in_dim(kh, cb * MOBA_BLOCK, MOBA_BLOCK, axis=2)
        v_own = lax.dynamic_slice_in_dim(vh, cb * MOBA_BLOCK, MOBA_BLOCK, axis=2)
        own_pos = cb * MOBA_BLOCK + jnp.arange(MOBA_BLOCK)
        s_own = jnp.einsum('bhqd,bhpd->bhqp', qb, k_own).astype(jnp.float32) * scale
        s_own = jnp.where(own_pos[None, :] <= t[:, None], s_own, NEG_INF)
        p = jax.nn.softmax(jnp.concatenate([s_sel, s_own], axis=-1), axis=-1)
        p_sel = p[..., : n_sel * MOBA_BLOCK].reshape(b, nh, MOBA_QBLOCK, n_sel, MOBA_BLOCK).astype(v.dtype)
        p_own = p[..., n_sel * MOBA_BLOCK:].astype(v.dtype)
        return jnp.einsum('bhqjp,bhqjpd->bhqd', p_sel, vg) + jnp.einsum('bhqp,bhpd->bhqd', p_own, v_own)

    out = lax.map(attend_block, (qc, jnp.arange(nq)))
    return out.transpose(1, 0, 3, 2, 4).reshape(b, s, nh, hd)


def moba_sample(q, kn, vn, cache_k, cache_v, page_table):
    db, L, nh, hd = q.shape
    n_past_pages = page_table.shape[1]
    page = cache_k.shape[1]
    past_len = n_past_pages * page
    ppb = MOBA_BLOCK // page
    nbt = -(-(past_len + L) // MOBA_BLOCK)
    t = past_len + jnp.arange(L)
    cb = t // MOBA_BLOCK
    page_sums = jnp.sum(cache_k, axis=1, dtype=jnp.float32)[page_table]
    page_sums = jnp.pad(page_sums, ((0, 0), (0, nbt * ppb - n_past_pages), (0, 0), (0, 0)))
    blk_sum = page_sums.reshape(db, nbt, ppb, nh, hd).sum(axis=2).at[:, cb].add(kn.astype(jnp.float32))
    kmean = blk_sum / MOBA_BLOCK
    s_blk = jnp.einsum('blhd,bnhd->blhn', q.astype(jnp.float32), kmean)
    s_blk = jnp.where((jnp.arange(nbt)[None, :] < cb[:, None])[None, :, None, :], s_blk, NEG_INF)
    n_sel = min(MOBA_TOPK, nbt)
    _, idx = lax.top_k(s_blk, n_sel)
    blocks = jnp.concatenate([idx, jnp.broadcast_to(cb[None, :, None, None], (db, L, nh, 1)).astype(idx.dtype)], axis=-1)
    valid_blk = jnp.concatenate([jnp.arange(n_sel)[None, :] < cb[:, None], jnp.ones((L, 1), bool)], axis=-1)
    n_cand = (n_sel + 1) * ppb
    lpage = (blocks[..., None] * ppb + jnp.arange(ppb)).reshape(db, L, nh, n_cand)
    page_ok = (lpage < n_past_pages) & jnp.repeat(valid_blk, ppb, axis=1)[None, :, None, :]
    phys = page_table[jnp.arange(db)[:, None, None, None], jnp.clip(lpage, 0, n_past_pages - 1)]
    hi = jnp.arange(nh)[None, None, :, None]
    kg = cache_k[phys, :, hi]
    vg = cache_v[phys, :, hi]
    scale = hd ** -0.5
    s_past = jnp.einsum('blhd,blhcpd->blhcp', q, kg).astype(jnp.float32) * scale
    s_past = jnp.where(page_ok[..., None], s_past, NEG_INF).reshape(db, L, nh, n_cand * page)
    member = jnp.any((blocks[..., None] == cb) & valid_blk[None, :, None, :, None], axis=3)
    causal = jnp.arange(L)[None, :] <= jnp.arange(L)[:, None]
    s_new = jnp.einsum('blhd,bmhd->blhm', q, kn).astype(jnp.float32) * scale
    s_new = jnp.where(member & causal[None, :, None, :], s_new, NEG_INF)
    p = jax.nn.softmax(jnp.concatenate([s_past, s_new], axis=-1), axis=-1)
    p_past = p[..., : n_cand * page].reshape(db, L, nh, n_cand, page).astype(vg.dtype)
    p_new = p[..., n_cand * page:].astype(vn.dtype)
    return jnp.einsum('blhcp,blhcpd->blhd', p_past, vg) + jnp.einsum('blhm,bmhd->blhd', p_new, vn)


def decoder(x, pos, ret_fn, attend_fn, norm_ffa, ffa_w_in, ffa_w_out, norm_mix, norm_ffb, ffb_w_in, ffb_w_out,
            ret_w_in, ret_w_out, kv_norm, w_kv, moba_w_q, moba_w_o, final_norm):
    b, s, _ = x.shape
    h = x
    ret_states = []
    k_sh = None
    v_sh = None
    for l in range(DEPTH):
        h = h + 0.5 * swiglu(rms_norm(h, norm_ffa[l]), ffa_w_in[l], ffa_w_out[l])
        u = rms_norm(h, norm_mix[l])
        if l < N_A_LAYERS:
            proj = u @ ret_w_in[l]
            q = proj[..., :RET_QK_W].reshape(b, s, RET_HEADS, RET_KEY_DIM)
            k = proj[..., RET_QK_W:2 * RET_QK_W].reshape(b, s, RET_HEADS, RET_KEY_DIM)
            v = proj[..., 2 * RET_QK_W:2 * RET_QK_W + RET_V_W].reshape(b, s, RET_HEADS, RET_VAL_DIM)
            g = proj[..., 2 * RET_QK_W + RET_V_W:]
            q = rope(q, pos)
            k = rope(k, pos) * (RET_KEY_DIM ** -0.5)
            o, st = ret_fn(l, q, k, v)
            ret_states.append(st)
            o = (head_group_norm(o).reshape(b, s, RET_V_W) * jax.nn.silu(g.astype(jnp.float32))).astype(x.dtype)
            h = h + o @ ret_w_out[l]
        else:
            j = l - N_A_LAYERS
            q = rope((u @ moba_w_q[j]).reshape(b, s, ATT_HEADS, HEAD_DIM), pos)
            o = attend_fn(q, k_sh, v_sh)
            h = h + o.reshape(b, s, ATT_W) @ moba_w_o[j]
        h = h + 0.5 * swiglu(rms_norm(h, norm_ffb[l]), ffb_w_in[l], ffb_w_out[l])
        if l == N_A_LAYERS - 1:
            kv = rms_norm(h, kv_norm) @ w_kv
            k_sh = rope(kv[..., :ATT_W].reshape(b, s, ATT_HEADS, HEAD_DIM), pos)
            v_sh = kv[..., ATT_W:].reshape(b, s, ATT_HEADS, HEAD_DIM)
    y = rms_norm(h, final_norm)
    return y, jnp.stack(ret_states).astype(x.dtype), k_sh, v_sh


def setup_inputs(seed: int = 0) -> dict:
    key = jax.random.key(seed)
    ks = jax.random.split(key, 24)
    f32 = jnp.float32

    def w(k, shape, fan_in):
        return jax.random.normal(k, shape, f32) * fan_in ** -0.5

    def gain(k, shape):
        return 1.0 + 0.02 * jax.random.normal(k, shape, f32)

    n_used = DEC_BATCH * N_PAGES
    page_table = jax.random.permutation(ks[5], N_POOL_PAGES)[:n_used].reshape(DEC_BATCH, N_PAGES).astype(jnp.int32)
    return {
        "x_prompt": jax.random.normal(ks[0], (BATCH, SEQ, D_MODEL), f32),
        "x_sample": jax.random.normal(ks[1], (DEC_BATCH, DEC_SEQ, D_MODEL), f32),
        "state_ret": 0.5 * jax.random.normal(ks[2], (N_A_LAYERS, DEC_BATCH, RET_HEADS, RET_KEY_DIM, RET_VAL_DIM), f32),
        "cache_k": jax.random.normal(ks[3], (N_POOL_PAGES, PAGE_SIZE, ATT_HEADS, HEAD_DIM), f32),
        "cache_v": jax.random.normal(ks[4], (N_POOL_PAGES, PAGE_SIZE, ATT_HEADS, HEAD_DIM), f32),
        "page_table": page_table,
        "norm_ffa": gain(ks[6], (DEPTH, D_MODEL)),
        "ffa_w_in": w(ks[7], (DEPTH, D_MODEL, 2 * FFN_DIM), D_MODEL),
        "ffa_w_out": w(ks[8], (DEPTH, FFN_DIM, D_MODEL), FFN_DIM),
        "norm_mix": gain(ks[9], (DEPTH, D_MODEL)),
        "norm_ffb": gain(ks[10], (DEPTH, D_MODEL)),
        "ffb_w_in": w(ks[11], (DEPTH, D_MODEL, 2 * FFN_DIM), D_MODEL),
        "ffb_w_out": w(ks[12], (DEPTH, FFN_DIM, D_MODEL), FFN_DIM),
        "ret_w_in": w(ks[13], (N_A_LAYERS, D_MODEL, RET_IN_W), D_MODEL),
        "ret_w_out": w(ks[14], (N_A_LAYERS, RET_V_W, D_MODEL), RET_V_W),
        "kv_norm": gain(ks[15], (D_MODEL,)),
        "w_kv": w(ks[16], (D_MODEL, 2 * ATT_W), D_MODEL),
        "moba_w_q": w(ks[17], (N_B_LAYERS, D_MODEL, ATT_W), D_MODEL),
        "moba_w_o": w(ks[18], (N_B_LAYERS, ATT_W, D_MODEL), ATT_W),
        "final_norm": gain(ks[19], (D_MODEL,)),
    }


def reference(x_prompt, x_sample, state_ret, cache_k, cache_v, page_table, norm_ffa, ffa_w_in, ffa_w_out,
              norm_mix, norm_ffb, ffb_w_in, ffb_w_out, ret_w_in, ret_w_out, kv_norm, w_kv, moba_w_q, moba_w_o,
              final_norm):
    log_g = retention_log_decay()
    pos_prompt = jnp.arange(x_prompt.shape[1], dtype=jnp.int32)
    past_len = page_table.shape[1] * cache_k.shape[1]
    pos_sample = past_len + jnp.arange(x_sample.shape[1], dtype=jnp.int32)

    def ret_prompt_fn(l, q, k, v):
        return retention_prompt(q, k, v, log_g)

    def ret_sample_fn(l, q, k, v):
        return retention_chunk(q, k, v, state_ret[l].astype(jnp.float32), log_g)

    def attend_sample(q, k, v):
        return moba_sample(q, k, v, cache_k, cache_v, page_table)

    y_prompt, state_ret_prompt, k_prompt, v_prompt = decoder(
        x_prompt, pos_prompt, ret_prompt_fn, moba_prompt, norm_ffa, ffa_w_in, ffa_w_out, norm_mix, norm_ffb,
        ffb_w_in, ffb_w_out, ret_w_in, ret_w_out, kv_norm, w_kv, moba_w_q, moba_w_o, final_norm)
    y_sample, state_ret_sample, k_sample, v_sample = decoder(
        x_sample, pos_sample, ret_sample_fn, attend_sample, norm_ffa, ffa_w_in, ffa_w_out, norm_mix, norm_ffb,
        ffb_w_in, ffb_w_out, ret_w_in, ret_w_out, kv_norm, w_kv, moba_w_q, moba_w_o, final_norm)
    return (y_prompt, y_sample, state_ret_prompt, state_ret_sample, k_prompt, v_prompt, k_sample, v_sample)
```

```python
import functools

import jax
import jax.numpy as jnp
from jax import lax
from jax.experimental import pallas as pl
from jax.experimental.pallas import tpu as pltpu

F32 = jnp.float32
BF16 = jnp.bfloat16

NORM_EPS = 1e-6
GN_EPS = 1e-5
NEG_INF = -1e30
ROPE_THETA = 10000.0
MOBA_BLOCK = 256
MOBA_TOPK = 3
RET_CHUNK = 128
LANES = 128
VMEM_LIMIT = 56 * 1024 * 1024


def _params(*sem):
    return pltpu.CompilerParams(dimension_semantics=sem, vmem_limit_bytes=VMEM_LIMIT)


def _dot(a, b):
    return jnp.dot(a, b, preferred_element_type=F32)


def _dot_nt(a, b):
    return lax.dot_general(a, b, (((1,), (1,)), ((), ())), preferred_element_type=F32)


def _dot_tn(a, b):
    return lax.dot_general(a, b, (((0,), (0,)), ((), ())), preferred_element_type=F32)


def _split_hi_lo(x):
    hi = x.astype(BF16)
    lo = (x - hi.astype(F32)).astype(BF16)
    return hi, lo


def _silu(x):
    return x * jax.nn.sigmoid(x)


def _row_tile(m):
    for t in (640, 512, 256, 128, 64, 32, 16):
        if m % t == 0:
            return t
    raise ValueError(f"row count {m} has no supported tile")


def _col_tile(n):
    for t in (512, 256, 128):
        if n % t == 0:
            return t
    raise ValueError(f"column count {n} has no supported tile")


def _rms(x, g):
    return x * lax.rsqrt(jnp.mean(x * x, axis=-1, keepdims=True) + NORM_EPS) * g


def _rope_store(o_ref, acc, cos, sin, d_head, scale):
    tn = acc.shape[1]
    half = d_head // 2
    for h in range(tn // d_head):
        lo = h * d_head
        if half % LANES == 0:
            x1 = acc[:, lo:lo + half]
            x2 = acc[:, lo + half:lo + d_head]
            r1 = x1 * cos - x2 * sin
            r2 = x2 * cos + x1 * sin
            if scale != 1.0:
                r1, r2 = r1 * scale, r2 * scale
            o_ref[:, lo:lo + half] = r1.astype(o_ref.dtype)
            o_ref[:, lo + half:lo + d_head] = r2.astype(o_ref.dtype)
        else:
            xh = acc[:, lo:lo + d_head]
            r = xh * cos + pltpu.roll(xh, half, 1) * sin
            if scale != 1.0:
                r = r * scale
            o_ref[:, lo:lo + d_head] = r.astype(o_ref.dtype)


def _norm_mm_body(mode, d_head, scale, x_ref, g_ref, *refs):
    xn_ref = refs[-1]

    @pl.when(pl.program_id(1) == 0)
    def _():
        xn_ref[...] = _rms(x_ref[...], g_ref[...]).astype(BF16)

    xn = xn_ref[...]
    if mode == "swiglu":
        wg_ref, wu_ref, o_ref = refs[:3]
        gate = _dot(xn, wg_ref[...])
        up = _dot(xn, wu_ref[...])
        o_ref[...] = (_silu(gate) * up).astype(o_ref.dtype)
    elif mode == "rope":
        w_ref, cos_ref, sin_ref, o_ref = refs[:4]
        _rope_store(o_ref, _dot(xn, w_ref[...]), cos_ref[...], sin_ref[...], d_head, scale)
    else:
        w_ref, o_ref = refs[:2]
        o_ref[...] = _dot(xn, w_ref[...]).astype(o_ref.dtype)


def _norm_matmul(x, g, w, out_dtype, mode="plain", cos=None, sin=None, d_head=0, scale=1.0):
    m, d = x.shape
    n = w.shape[1] // 2 if mode == "swiglu" else w.shape[1]
    tm, tn = _row_tile(m), _col_tile(n)
    nj = n // tn
    in_specs = [pl.BlockSpec((tm, d), lambda i, j: (i, 0)), pl.BlockSpec((1, d), lambda i, j: (0, 0))]
    args = [x, g.reshape(1, d)]
    if mode == "swiglu":
        in_specs += [pl.BlockSpec((d, tn), lambda i, j: (0, j)), pl.BlockSpec((d, tn), lambda i, j: (0, j + nj))]
        args += [w, w]
    else:
        in_specs.append(pl.BlockSpec((d, tn), lambda i, j: (0, j)))
        args.append(w)
    if mode == "rope":
        assert tn % d_head == 0
        in_specs += [pl.BlockSpec((tm, LANES), lambda i, j: (i, 0))] * 2
        args += [cos, sin]
    return pl.pallas_call(
        functools.partial(_norm_mm_body, mode, d_head, scale),
        out_shape=jax.ShapeDtypeStruct((m, n), out_dtype),
        grid=(m // tm, nj),
        in_specs=in_specs,
        out_specs=pl.BlockSpec((tm, tn), lambda i, j: (i, j)),
        scratch_shapes=[pltpu.VMEM((tm, d), BF16)],
        compiler_params=_params("parallel", "arbitrary"),
        name=f"norm_matmul_{mode}",
    )(*args)


def _mm_res_body(scale, a_ref, w_ref, h_ref, o_ref):
    acc = _dot(a_ref[...], w_ref[...])
    o_ref[...] = h_ref[...] + (acc if scale == 1.0 else scale * acc)


def _matmul_residual(a, w, h, scale):
    m, k = a.shape
    n = w.shape[1]
    tm, tn = _row_tile(m), _col_tile(n)
    return pl.pallas_call(
        functools.partial(_mm_res_body, scale),
        out_shape=jax.ShapeDtypeStruct((m, n), F32),
        grid=(m // tm, n // tn),
        in_specs=[pl.BlockSpec((tm, k), lambda i, j: (i, 0)),
                  pl.BlockSpec((k, tn), lambda i, j: (0, j)),
                  pl.BlockSpec((tm, tn), lambda i, j: (i, j))],
        out_specs=pl.BlockSpec((tm, tn), lambda i, j: (i, j)),
        compiler_params=_params("parallel", "arbitrary"),
        name="matmul_residual",
    )(a, w, h)


def _rmsnorm_body(x_ref, g_ref, o_ref):
    o_ref[...] = _rms(x_ref[...], g_ref[...])


def _rmsnorm(x, g):
    m, d = x.shape
    tm = _row_tile(m)
    return pl.pallas_call(
        _rmsnorm_body,
        out_shape=jax.ShapeDtypeStruct((m, d), F32),
        grid=(m // tm,),
        in_specs=[pl.BlockSpec((tm, d), lambda i: (i, 0)), pl.BlockSpec((1, d), lambda i: (0, 0))],
        out_specs=pl.BlockSpec((tm, d), lambda i: (i, 0)),
        compiler_params=_params("parallel"),
        name="final_rmsnorm",
    )(x, g.reshape(1, d))


def _group_norm_gate(o, g):
    mu = jnp.mean(o, axis=-1, keepdims=True)
    d = o - mu
    var = jnp.mean(d * d, axis=-1, keepdims=True)
    return d * lax.rsqrt(var + GN_EPS) * _silu(g)


def _ret_prompt_body(q_ref, k_ref, v_ref, g_ref, dmat_ref, cdec_ref, kdec_ref, gl_ref, o_ref, sfin_ref, s_scr):
    c = pl.program_id(1)

    @pl.when(c == 0)
    def _():
        s_scr[...] = jnp.zeros_like(s_scr)

    q = q_ref[...].astype(BF16)
    k = k_ref[...]
    v = v_ref[...]
    s = s_scr[...]
    a = _dot_nt(q, k.astype(BF16)) * dmat_ref[0]
    inner = _dot(a.astype(BF16), v)
    cross = _dot(q, s.astype(BF16)) * cdec_ref[0]
    kd = (k * kdec_ref[0]).astype(BF16)
    s_new = gl_ref[0] * s + _dot_tn(kd, v)
    s_scr[...] = s_new
    o_ref[...] = _group_norm_gate(inner + cross, g_ref[...].astype(F32)).astype(o_ref.dtype)

    @pl.when(c == pl.num_programs(1) - 1)
    def _():
        sfin_ref[0] = s_new


def _retention_prompt(q, k, vg, s_len, n_heads, dk, dv, log_g):
    L = RET_CHUNK
    nc = s_len // L
    i = jnp.arange(L, dtype=F32)
    diff = i[:, None] - i[None, :]
    dmat = jnp.where(diff >= 0, jnp.exp(log_g[:, None, None] * jnp.maximum(diff, 0.0)), 0.0)
    cdec = jnp.exp(log_g[:, None] * (i[None, :] + 1.0))[:, :, None]
    kdec = jnp.exp(log_g[:, None] * (L - 1.0 - i)[None, :])[:, :, None]
    gl = jnp.broadcast_to(jnp.exp(log_g * L)[:, None, None], (n_heads, 1, dv))
    return pl.pallas_call(
        _ret_prompt_body,
        out_shape=(jax.ShapeDtypeStruct((s_len, n_heads * dv), BF16),
                   jax.ShapeDtypeStruct((n_heads, dk, dv), F32)),
        grid=(n_heads, nc),
        in_specs=[pl.BlockSpec((L, dk), lambda h, c: (c, h)),
                  pl.BlockSpec((L, dk), lambda h, c: (c, h)),
                  pl.BlockSpec((L, dv), lambda h, c: (c, h)),
                  pl.BlockSpec((L, dv), lambda h, c: (c, h + n_heads)),
                  pl.BlockSpec((1, L, L), lambda h, c: (h, 0, 0)),
                  pl.BlockSpec((1, L, 1), lambda h, c: (h, 0, 0)),
                  pl.BlockSpec((1, L, 1), lambda h, c: (h, 0, 0)),
                  pl.BlockSpec((1, 1, dv), lambda h, c: (h, 0, 0))],
        out_specs=(pl.BlockSpec((L, dv), lambda h, c: (c, h)),
                   pl.BlockSpec((1, dk, dv), lambda h, c: (h, 0, 0))),
        scratch_shapes=[pltpu.VMEM((dk, dv), F32)],
        compiler_params=_params("parallel", "arbitrary"),
        name="retention_prompt",
    )(q, k, vg, vg, dmat, cdec, kdec, gl)


def _ret_sample_body(n_heads, qc_ref, kc_ref, v_ref, g_ref, g1_ref, s_ref, o_ref, snew_ref):
    qc = qc_ref[0]
    kc = kc_ref[0]
    for h in range(n_heads):
        s = s_ref[0, h]
        qh = qc[:, h:h + 1]
        kh = kc[:, h:h + 1]
        vh = v_ref[0, h:h + 1, :]
        g1 = g1_ref[h:h + 1, :]
        a = jnp.sum(qh * kh, axis=0, keepdims=True)
        cross = jnp.sum(s * qh, axis=0, keepdims=True) * g1
        snew_ref[0, h] = g1 * s + kh * vh
        o = a * vh + cross
        o_ref[0, h:h + 1, :] = _group_norm_gate(o, g_ref[0, h:h + 1, :])


def _retention_sample(q, k, v, g, state, log_g):
    b, n_heads, dk, dv = state.shape
    qc = q.reshape(b, n_heads, dk).transpose(0, 2, 1)
    kc = k.reshape(b, n_heads, dk).transpose(0, 2, 1)
    g1 = jnp.broadcast_to(jnp.exp(log_g * 1.0)[:, None], (n_heads, dv))
    return pl.pallas_call(
        functools.partial(_ret_sample_body, n_heads),
        out_shape=(jax.ShapeDtypeStruct((b, n_heads, dv), F32),
                   jax.ShapeDtypeStruct((b, n_heads, dk, dv), F32)),
        grid=(b,),
        in_specs=[pl.BlockSpec((1, dk, n_heads), lambda i: (i, 0, 0)),
                  pl.BlockSpec((1, dk, n_heads), lambda i: (i, 0, 0)),
                  pl.BlockSpec((1, n_heads, dv), lambda i: (i, 0, 0)),
                  pl.BlockSpec((1, n_heads, dv), lambda i: (i, 0, 0)),
                  pl.BlockSpec((n_heads, dv), lambda i: (0, 0)),
                  pl.BlockSpec((1, n_heads, dk, dv), lambda i: (i, 0, 0, 0))],
        out_specs=(pl.BlockSpec((1, n_heads, dv), lambda i: (i, 0, 0)),
                   pl.BlockSpec((1, n_heads, dk, dv), lambda i: (i, 0, 0, 0))),
        compiler_params=_params("parallel"),
        name="retention_sample",
    )(qc, kc, v.reshape(b, n_heads, dv), g.reshape(b, n_heads, dv), g1, state)


def _select_topk(s, idx, n_valid, axis):
    big = float(s.shape[axis])
    s = jnp.where(idx < n_valid, s, NEG_INF)
    sel = jnp.zeros(s.shape, F32)
    for j in range(min(MOBA_TOPK, s.shape[axis])):
        m = jnp.max(s, axis=axis, keepdims=True)
        first = jnp.min(jnp.where(s == m, idx, big), axis=axis, keepdims=True)
        pick = idx == first
        sel = jnp.where(pick, jnp.where(n_valid > j, 1.0, 0.0), sel)
        s = jnp.where(pick, -jnp.inf, s)
    return sel


def _moba_prompt_body(nb, scale, q_ref, k_ref, v_ref, o_ref, kmean_scr):
    blk = MOBA_BLOCK
    qb = pl.program_id(1)

    @pl.when(qb == 0)
    def _():
        for n in range(nb):
            kmean_scr[n:n + 1, :] = jnp.mean(k_ref[n * blk:(n + 1) * blk, :], axis=0, keepdims=True)

    q = q_ref[...]
    q_hi, q_lo = _split_hi_lo(q)
    km_hi, km_lo = _split_hi_lo(kmean_scr[...])
    s_blk = _dot_nt(q_lo, km_hi) + _dot_nt(q_hi, km_lo) + _dot_nt(q_hi, km_hi)
    col = lax.broadcasted_iota(jnp.int32, (blk, nb), 1)
    sel = _select_topk(s_blk, col.astype(F32), qb.astype(F32), axis=1)

    def scores(n):
        start = pl.multiple_of(n * blk, blk)
        kb = k_ref[pl.ds(start, blk), :].astype(BF16)
        vb = v_ref[pl.ds(start, blk), :].astype(BF16)
        return _dot_nt(q_hi, kb) * scale, vb

    s, vb = scores(qb)
    row = lax.broadcasted_iota(jnp.int32, (blk, blk), 0)
    cpos = lax.broadcasted_iota(jnp.int32, (blk, blk), 1)
    s = jnp.where(cpos <= row, s, NEG_INF)
    m0 = jnp.max(s, axis=1, keepdims=True)
    p = jnp.exp(s - m0)
    l0 = jnp.sum(p, axis=1, keepdims=True)
    acc0 = _dot(p.astype(BF16), vb)

    def body(n, carry):
        m, l, acc = carry
        s, vb = scores(n)
        chosen = jnp.sum(jnp.where(col == n, sel, 0.0), axis=1, keepdims=True)
        s = jnp.where(chosen > 0.5, s, NEG_INF)
        m_new = jnp.maximum(m, jnp.max(s, axis=1, keepdims=True))
        alpha = jnp.exp(m - m_new)
        p = jnp.exp(s - m_new)
        return m_new, alpha * l + jnp.sum(p, axis=1, keepdims=True), alpha * acc + _dot(p.astype(BF16), vb)

    m, l, acc = lax.fori_loop(0, qb, body, (m0, l0, acc0))
    o_ref[...] = (acc / l).astype(o_ref.dtype)


def _moba_prompt(q, k, v, s_len, n_heads, hd):
    assert s_len % MOBA_BLOCK == 0
    nb = s_len // MOBA_BLOCK
    return pl.pallas_call(
        functools.partial(_moba_prompt_body, nb, hd ** -0.5),
        out_shape=jax.ShapeDtypeStruct((s_len, n_heads * hd), BF16),
        grid=(n_heads, nb),
        in_specs=[pl.BlockSpec((MOBA_BLOCK, hd), lambda h, i: (i, h)),
                  pl.BlockSpec((s_len, hd), lambda h, i: (0, h)),
                  pl.BlockSpec((s_len, hd), lambda h, i: (0, h))],
        out_specs=pl.BlockSpec((MOBA_BLOCK, hd), lambda h, i: (i, h)),
        scratch_shapes=[pltpu.VMEM((nb, hd), F32)],
        compiler_params=_params("parallel", "arbitrary"),
        name="moba_prompt",
    )(q, k, v)


def _moba_sample_body(G, NG, page, n_heads, scale, pt_ref, *refs):
    ck = refs[:G]
    cv = refs[G:2 * G]
    qhi_ref, qlo_ref, kn_ref, vn_ref, e_ref, o_ref, s_all, psum, p_all, acc, pnew, kmean = refs[2 * G:]
    NP = G * NG
    ppb = MOBA_BLOCK // page
    cb = NP // ppb
    t = pl.program_id(1)

    @pl.when(t < NG)
    def _k_phase():
        for g in range(G):
            kp = ck[g][0]
            pg = t * G + g
            psum[pg] = jnp.sum(kp, axis=0, keepdims=True)
            s_all[pg] = _dot(kp.astype(BF16), qhi_ref[0])

    @pl.when(t == NG - 1)
    def _select_and_softmax():
        q_hi, q_lo = qhi_ref[0], qlo_ref[0]
        kn = kn_ref[0]
        kmean[...] = jnp.zeros_like(kmean)
        for n in range(cb):
            kmean[n:n + 1, :] = sum(psum[n * ppb + j] for j in range(ppb)) / MOBA_BLOCK
        kmean[cb:cb + 1, :] = kn / MOBA_BLOCK
        km_hi, km_lo = _split_hi_lo(kmean[...])
        s_blk = _dot(km_hi, q_lo) + _dot(km_lo, q_hi) + _dot(km_hi, q_hi)
        bidx = lax.broadcasted_iota(jnp.int32, s_blk.shape, 0).astype(F32)
        sel = _select_topk(s_blk, bidx, float(cb), axis=0)
        kn8 = jnp.broadcast_to(kn, (8, kn.shape[1])).astype(BF16)
        s_new = _dot(kn8, q_hi)[0:1] * scale
        m = s_new
        for pg in range(NP):
            n = pg // ppb
            sm = jnp.where(sel[n:n + 1, :] > 0.5, s_all[pg] * scale, NEG_INF)
            s_all[pg] = sm
            m = jnp.maximum(m, jnp.max(sm, axis=0, keepdims=True))
        l = jnp.exp(s_new - m)
        for pg in range(NP):
            p = jnp.exp(s_all[pg] - m)
            s_all[pg] = p
            l = l + jnp.sum(p, axis=0, keepdims=True)
        inv = 1.0 / l
        for pg in range(NP):
            p_all[pg] = (s_all[pg] * inv).astype(BF16)
        pnew[...] = jnp.broadcast_to(jnp.exp(s_new - m) * inv, pnew.shape)

    @pl.when(t == NG)
    def _():
        acc[...] = jnp.zeros_like(acc)

    @pl.when(t >= NG)
    def _v_phase():
        for g in range(G):
            pg = (t - NG) * G + g
            pexp = _dot(p_all[pg], e_ref[...])
            prod = pexp * cv[g][0]
            acc[...] += jnp.sum(prod.reshape(page // 8, 8, prod.shape[1]), axis=0)

    @pl.when(t == 2 * NG - 1)
    def _():
        pn = _dot(pnew[...].astype(BF16), e_ref[...])[0:1]
        o_ref[0] = jnp.sum(acc[...], axis=0, keepdims=True) + pn * vn_ref[0]


def _moba_sample(q, kn, vn, cache_k, cache_v, page_table):
    n_pool, page, n_heads, hd = cache_k.shape
    b, n_pages = page_table.shape
    da = n_heads * hd
    assert MOBA_BLOCK % page == 0 and n_pages % (MOBA_BLOCK // page) == 0 and n_heads <= LANES
    G = 4 if n_pages % 4 == 0 else (2 if n_pages % 2 == 0 else 1)
    NG = n_pages // G
    ck = cache_k.reshape(n_pool, page, da)
    cv = cache_v.reshape(n_pool, page, da)
    eye = jnp.eye(n_heads, LANES, dtype=F32)
    qbd = (q.reshape(b, n_heads, hd, 1) * eye[None, :, None, :]).reshape(b, da, LANES)
    qbd_hi, qbd_lo = _split_hi_lo(qbd)
    expand = jnp.repeat(jnp.eye(LANES, n_heads, dtype=BF16), hd, axis=1)

    def k_map(g):
        return lambda i, t, pt: (pt[i, jnp.minimum(t, NG - 1) * G + g], 0, 0)

    def v_map(g):
        return lambda i, t, pt: (pt[i, jnp.maximum(t - NG, 0) * G + g], 0, 0)

    per_b = lambda i, t, pt: (i, 0, 0)
    grid_spec = pltpu.PrefetchScalarGridSpec(
        num_scalar_prefetch=1,
        grid=(b, 2 * NG),
        in_specs=[pl.BlockSpec((1, page, da), k_map(g)) for g in range(G)]
        + [pl.BlockSpec((1, page, da), v_map(g)) for g in range(G)]
        + [pl.BlockSpec((1, da, LANES), per_b), pl.BlockSpec((1, da, LANES), per_b),
           pl.BlockSpec((1, 1, da), per_b), pl.BlockSpec((1, 1, da), per_b),
           pl.BlockSpec((LANES, da), lambda i, t, pt: (0, 0))],
        out_specs=pl.BlockSpec((1, 1, da), per_b),
        scratch_shapes=[pltpu.VMEM((n_pages, page, LANES), F32),
                        pltpu.VMEM((n_pages, 1, da), F32),
                        pltpu.VMEM((n_pages, page, LANES), BF16),
                        pltpu.VMEM((8, da), F32),
                        pltpu.VMEM((8, LANES), F32),
                        pltpu.VMEM((-(-(n_pages // (MOBA_BLOCK // page) + 1) // 8) * 8, da), F32)])
    out = pl.pallas_call(
        functools.partial(_moba_sample_body, G, NG, page, n_heads, hd ** -0.5),
        out_shape=jax.ShapeDtypeStruct((b, 1, da), F32),
        grid_spec=grid_spec,
        compiler_params=_params("parallel", "arbitrary"),
        name="moba_sample",
    )(page_table, *([ck] * G), *([cv] * G), qbd_hi, qbd_lo, kn.reshape(b, 1, da), vn.reshape(b, 1, da), expand)
    return out.reshape(b, da)


def _rope_tables(pos, d_head):
    inv = ROPE_THETA ** (-jnp.arange(0, d_head, 2, dtype=F32) / d_head)
    ang = pos.astype(F32)[:, None] * inv[None, :]
    cos, sin = jnp.cos(ang), jnp.sin(ang)
    if d_head // 2 == LANES:
        return cos, sin
    assert d_head == LANES
    return jnp.concatenate([cos, cos], axis=1), jnp.concatenate([-sin, sin], axis=1)


def _ffn(h, g, w_in, w_out):
    a = _norm_matmul(h, g, w_in.astype(BF16), BF16, mode="swiglu")
    return _matmul_residual(a, w_out.astype(BF16), h, 0.5)


def kernel(x_prompt, x_sample, state_ret, cache_k, cache_v, page_table, norm_ffa, ffa_w_in, ffa_w_out, norm_mix, norm_ffb, ffb_w_in, ffb_w_out, ret_w_in, ret_w_out, kv_norm, w_kv, moba_w_q, moba_w_o, final_norm):
    bp, s_len, d = x_prompt.shape
    db, dl, _ = x_sample.shape
    n_a, _, ret_heads, dk, dv = state_ret.shape
    _, page, att_heads, hd = cache_k.shape
    assert bp == 1 and dl == 1 and n_a == 1 and norm_ffa.shape[0] == 2 and s_len % db == 0
    assert s_len % RET_CHUNK == 0 and s_len % MOBA_BLOCK == 0
    past_len = page_table.shape[1] * page
    qk_w, v_w, att_w = ret_heads * dk, ret_heads * dv, att_heads * hd

    pos = jnp.concatenate([jnp.arange(s_len, dtype=jnp.int32), jnp.full((db,), past_len, jnp.int32)])
    cos_r, sin_r = _rope_tables(pos, dk)
    cos_m, sin_m = _rope_tables(pos, hd)
    log_g = jnp.log(1.0 - 2.0 ** (-5.0 - jnp.arange(ret_heads, dtype=F32)))

    h = jnp.concatenate([x_prompt[0], x_sample[:, 0]], axis=0)

    h = _ffn(h, norm_ffa[0], ffa_w_in[0], ffa_w_out[0])
    w_ret = ret_w_in[0]
    q = _norm_matmul(h, norm_mix[0], w_ret[:, :qk_w].astype(BF16), F32, "rope", cos_r, sin_r, dk)
    k = _norm_matmul(h, norm_mix[0], w_ret[:, qk_w:2 * qk_w].astype(BF16), F32, "rope", cos_r, sin_r, dk, dk ** -0.5)
    vg = _norm_matmul(h, norm_mix[0], w_ret[:, 2 * qk_w:].astype(BF16), BF16)
    o_p, st_p = _retention_prompt(q, k, vg, s_len, ret_heads, dk, dv, log_g)
    vg_s = vg[s_len:].astype(F32)
    o_s, st_s = _retention_sample(q[s_len:], k[s_len:], vg_s[:, :v_w], vg_s[:, v_w:], state_ret[0], log_g)
    o = jnp.concatenate([o_p, o_s.reshape(db, v_w).astype(BF16)], axis=0)
    h = _matmul_residual(o, ret_w_out[0].astype(BF16), h, 1.0)
    h = _ffn(h, norm_ffb[0], ffb_w_in[0], ffb_w_out[0])

    k_sh = _norm_matmul(h, kv_norm, w_kv[:, :att_w].astype(BF16), F32, "rope", cos_m, sin_m, hd)
    v_sh = _norm_matmul(h, kv_norm, w_kv[:, att_w:].astype(BF16), F32)

    h = _ffn(h, norm_ffa[1], ffa_w_in[1], ffa_w_out[1])
    q = _norm_matmul(h, norm_mix[1], moba_w_q[0].astype(BF16), F32, "rope", cos_m, sin_m, hd)
    a_p = _moba_prompt(q, k_sh, v_sh, s_len, att_heads, hd)
    a_s = _moba_sample(q[s_len:], k_sh[s_len:], v_sh[s_len:], cache_k, cache_v, page_table)
    a = jnp.concatenate([a_p, a_s.astype(BF16)], axis=0)
    h = _matmul_residual(a, moba_w_o[0].astype(BF16), h, 1.0)
    h = _ffn(h, norm_ffb[1], ffb_w_in[1], ffb_w_out[1])
    y = _rmsnorm(h, final_norm)

    return (y[:s_len].reshape(1, s_len, d),
            y[s_len:].reshape(db, 1, d),
            st_p[None, None],
            st_s[None],
            k_sh[:s_len].reshape(1, s_len, att_heads, hd),
            v_sh[:s_len].reshape(1, s_len, att_heads, hd),
            k_sh[s_len:].reshape(db, 1, att_heads, hd),
            v_sh[s_len:].reshape(db, 1, att_heads, hd))
```

```python
import functools

import jax
import jax.numpy as jnp
from jax import lax
from jax.experimental import pallas as pl
from jax.experimental.pallas import tpu as pltpu

F32 = jnp.float32
BF16 = jnp.bfloat16

NORM_EPS = 1e-6
GN_EPS = 1e-5
NEG_INF = -1e30
ROPE_THETA = 10000.0
MOBA_BLOCK = 256
MOBA_TOPK = 3
RET_CHUNK = 128
LOG2_E = 1.4426950408889634
LANES = 128
VMEM_LIMIT = 56 * 1024 * 1024


def _params(*sem):
    return pltpu.CompilerParams(dimension_semantics=sem, vmem_limit_bytes=VMEM_LIMIT)


def _dot(a, b):
    return jnp.dot(a, b, preferred_element_type=F32)


def _dot_nt(a, b):
    return lax.dot_general(a, b, (((1,), (1,)), ((), ())), preferred_element_type=F32)


def _dot_tn(a, b):
    return lax.dot_general(a, b, (((0,), (0,)), ((), ())), preferred_element_type=F32)


def _split_hi_lo(x):
    hi = x.astype(BF16)
    lo = (x - hi.astype(F32)).astype(BF16)
    return hi, lo


def _silu(x):
    return x * jax.nn.sigmoid(x)


def _row_tile(m):
    for t in (640, 512, 256, 128, 64, 32, 16):
        if m % t == 0:
            return t
    raise ValueError(f"row count {m} has no supported tile")


def _col_tile(n, cap=1024):
    for t in (1024, 512, 256, 128):
        if t > cap:
            continue
        if n % t == 0:
            return t
    raise ValueError(f"column count {n} has no supported tile")


def _rms(x, g):
    return x * lax.rsqrt(jnp.mean(x * x, axis=-1, keepdims=True) + NORM_EPS) * g


def _rope_store(o_ref, acc, cos, sin, d_head, scale):
    tn = acc.shape[1]
    half = d_head // 2
    for h in range(tn // d_head):
        lo = h * d_head
        if half % LANES == 0:
            x1 = acc[:, lo:lo + half]
            x2 = acc[:, lo + half:lo + d_head]
            r1 = x1 * cos - x2 * sin
            r2 = x2 * cos + x1 * sin
            if scale != 1.0:
                r1, r2 = r1 * scale, r2 * scale
            o_ref[:, lo:lo + half] = r1.astype(o_ref.dtype)
            o_ref[:, lo + half:lo + d_head] = r2.astype(o_ref.dtype)
        else:
            xh = acc[:, lo:lo + d_head]
            r = xh * cos + pltpu.roll(xh, half, 1) * sin
            if scale != 1.0:
                r = r * scale
            o_ref[:, lo:lo + d_head] = r.astype(o_ref.dtype)


def _norm_mm_body(mode, d_head, scale, x_ref, g_ref, *refs):
    xn_ref = refs[-1]

    @pl.when(pl.program_id(1) == 0)
    def _():
        xn_ref[...] = _rms(x_ref[...], g_ref[...]).astype(BF16)

    xn = xn_ref[...]
    if mode == "swiglu":
        wg_ref, wu_ref, o_ref = refs[:3]
        gate = _dot(xn, wg_ref[...])
        up = _dot(xn, wu_ref[...])
        o_ref[...] = (_silu(gate) * up).astype(o_ref.dtype)
    elif mode == "rope":
        w_ref, cos_ref, sin_ref, o_ref = refs[:4]
        _rope_store(o_ref, _dot(xn, w_ref[...]), cos_ref[...], sin_ref[...], d_head, scale)
    else:
        w_ref, o_ref = refs[:2]
        o_ref[...] = _dot(xn, w_ref[...]).astype(o_ref.dtype)


def _norm_matmul(x, g, w, out_dtype, mode="plain", cos=None, sin=None, d_head=0, scale=1.0):
    m, d = x.shape
    n = w.shape[1] // 2 if mode == "swiglu" else w.shape[1]
    tm, tn = _row_tile(m), _col_tile(n)
    nj = n // tn
    in_specs = [pl.BlockSpec((tm, d), lambda i, j: (i, 0)), pl.BlockSpec((1, d), lambda i, j: (0, 0))]
    args = [x, g.reshape(1, d)]
    if mode == "swiglu":
        in_specs += [pl.BlockSpec((d, tn), lambda i, j: (0, j)), pl.BlockSpec((d, tn), lambda i, j: (0, j + nj))]
        args += [w, w]
    else:
        in_specs.append(pl.BlockSpec((d, tn), lambda i, j: (0, j)))
        args.append(w)
    if mode == "rope":
        assert tn % d_head == 0
        in_specs += [pl.BlockSpec((tm, LANES), lambda i, j: (i, 0))] * 2
        args += [cos, sin]
    return pl.pallas_call(
        functools.partial(_norm_mm_body, mode, d_head, scale),
        out_shape=jax.ShapeDtypeStruct((m, n), out_dtype),
        grid=(m // tm, nj),
        in_specs=in_specs,
        out_specs=pl.BlockSpec((tm, tn), lambda i, j: (i, j)),
        scratch_shapes=[pltpu.VMEM((tm, d), BF16)],
        compiler_params=_params("parallel", "arbitrary"),
        name=f"norm_matmul_{mode}",
    )(*args)


def _mm_res_body(scale, a_ref, w_ref, h_ref, o_ref):
    acc = _dot(a_ref[...], w_ref[...])
    o_ref[...] = h_ref[...] + (acc if scale == 1.0 else scale * acc)


def _matmul_residual(a, w, h, scale):
    m, k = a.shape
    n = w.shape[1]
    tm, tn = _row_tile(m), _col_tile(n, cap=512)
    return pl.pallas_call(
        functools.partial(_mm_res_body, scale),
        out_shape=jax.ShapeDtypeStruct((m, n), F32),
        grid=(m // tm, n // tn),
        in_specs=[pl.BlockSpec((tm, k), lambda i, j: (i, 0)),
                  pl.BlockSpec((k, tn), lambda i, j: (0, j)),
                  pl.BlockSpec((tm, tn), lambda i, j: (i, j))],
        out_specs=pl.BlockSpec((tm, tn), lambda i, j: (i, j)),
        compiler_params=_params("parallel", "arbitrary"),
        name="matmul_residual",
    )(a, w, h)


def _rmsnorm_body(x_ref, g_ref, o_ref):
    o_ref[...] = _rms(x_ref[...], g_ref[...])


def _rmsnorm(x, g):
    m, d = x.shape
    tm = _row_tile(m)
    return pl.pallas_call(
        _rmsnorm_body,
        out_shape=jax.ShapeDtypeStruct((m, d), F32),
        grid=(m // tm,),
        in_specs=[pl.BlockSpec((tm, d), lambda i: (i, 0)), pl.BlockSpec((1, d), lambda i: (0, 0))],
        out_specs=pl.BlockSpec((tm, d), lambda i: (i, 0)),
        compiler_params=_params("parallel"),
        name="final_rmsnorm",
    )(x, g.reshape(1, d))


def _group_norm_gate(o, g):
    mu = jnp.mean(o, axis=-1, keepdims=True)
    d = o - mu
    var = jnp.mean(d * d, axis=-1, keepdims=True)
    return d * lax.rsqrt(var + GN_EPS) * _silu(g)


def _ret_prompt_body(q_ref, k_ref, v_ref, g_ref, dmat_ref, cdec_ref, kdec_ref, gl_ref, o_ref, sfin_ref, s_scr):
    c = pl.program_id(1)

    @pl.when(c == 0)
    def _():
        s_scr[...] = jnp.zeros_like(s_scr)

    L = dmat_ref.shape[1]
    s = s_scr[...]
    for u in range(q_ref.shape[0] // L):
        rows = slice(u * L, (u + 1) * L)
        q = q_ref[rows, :].astype(BF16)
        k = k_ref[rows, :]
        v = v_ref[rows, :]
        a = _dot_nt(q, k.astype(BF16)) * dmat_ref[0]
        inner = _dot(a.astype(BF16), v)
        cross = _dot(q, s.astype(BF16)) * cdec_ref[0]
        kd = (k * kdec_ref[0]).astype(BF16)
        s = gl_ref[0] * s + _dot_tn(kd, v)
        o_ref[rows, :] = _group_norm_gate(inner + cross, g_ref[rows, :].astype(F32)).astype(o_ref.dtype)
    s_scr[...] = s

    @pl.when(c == pl.num_programs(1) - 1)
    def _():
        sfin_ref[0] = s


def _retention_prompt(q, k, vg, s_len, n_heads, dk, dv, log_g):
    L = RET_CHUNK
    per_step = 4 if s_len % (4 * L) == 0 else 1
    T = per_step * L
    i = jnp.arange(L, dtype=F32)
    diff = i[:, None] - i[None, :]
    dmat = jnp.where(diff >= 0, jnp.exp(log_g[:, None, None] * jnp.maximum(diff, 0.0)), 0.0)
    cdec = jnp.exp(log_g[:, None] * (i[None, :] + 1.0))[:, :, None]
    kdec = jnp.exp(log_g[:, None] * (L - 1.0 - i)[None, :])[:, :, None]
    gl = jnp.broadcast_to(jnp.exp(log_g * L)[:, None, None], (n_heads, 1, dv))
    return pl.pallas_call(
        _ret_prompt_body,
        out_shape=(jax.ShapeDtypeStruct((s_len, n_heads * dv), BF16),
                   jax.ShapeDtypeStruct((n_heads, dk, dv), F32)),
        grid=(n_heads, s_len // T),
        in_specs=[pl.BlockSpec((T, dk), lambda h, c: (c, h)),
                  pl.BlockSpec((T, dk), lambda h, c: (c, h)),
                  pl.BlockSpec((T, dv), lambda h, c: (c, h)),
                  pl.BlockSpec((T, dv), lambda h, c: (c, h + n_heads)),
                  pl.BlockSpec((1, L, L), lambda h, c: (h, 0, 0)),
                  pl.BlockSpec((1, L, 1), lambda h, c: (h, 0, 0)),
                  pl.BlockSpec((1, L, 1), lambda h, c: (h, 0, 0)),
                  pl.BlockSpec((1, 1, dv), lambda h, c: (h, 0, 0))],
        out_specs=(pl.BlockSpec((T, dv), lambda h, c: (c, h)),
                   pl.BlockSpec((1, dk, dv), lambda h, c: (h, 0, 0))),
        scratch_shapes=[pltpu.VMEM((dk, dv), F32)],
        compiler_params=_params("parallel", "arbitrary"),
        name="retention_prompt",
    )(q, k, vg, vg, dmat, cdec, kdec, gl)


def _ret_sample_body(n_heads, qc_ref, kc_ref, v_ref, g_ref, g1_ref, s_ref, o_ref, snew_ref):
    qc = qc_ref[0]
    kc = kc_ref[0]
    for h in range(n_heads):
        s = s_ref[0, h]
        qh = qc[:, h:h + 1]
        kh = kc[:, h:h + 1]
        vh = v_ref[0, h:h + 1, :]
        g1 = g1_ref[h:h + 1, :]
        a = jnp.sum(qh * kh, axis=0, keepdims=True)
        cross = jnp.sum(s * qh, axis=0, keepdims=True) * g1
        snew_ref[0, h] = g1 * s + kh * vh
        o = a * vh + cross
        o_ref[0, h:h + 1, :] = _group_norm_gate(o, g_ref[0, h:h + 1, :])


def _retention_sample(q, k, v, g, state, log_g):
    b, n_heads, dk, dv = state.shape
    qc = q.reshape(b, n_heads, dk).transpose(0, 2, 1)
    kc = k.reshape(b, n_heads, dk).transpose(0, 2, 1)
    g1 = jnp.broadcast_to(jnp.exp(log_g * 1.0)[:, None], (n_heads, dv))
    return pl.pallas_call(
        functools.partial(_ret_sample_body, n_heads),
        out_shape=(jax.ShapeDtypeStruct((b, n_heads, dv), F32),
                   jax.ShapeDtypeStruct((b, n_heads, dk, dv), F32)),
        grid=(b,),
        in_specs=[pl.BlockSpec((1, dk, n_heads), lambda i: (i, 0, 0)),
                  pl.BlockSpec((1, dk, n_heads), lambda i: (i, 0, 0)),
                  pl.BlockSpec((1, n_heads, dv), lambda i: (i, 0, 0)),
                  pl.BlockSpec((1, n_heads, dv), lambda i: (i, 0, 0)),
                  pl.BlockSpec((n_heads, dv), lambda i: (0, 0)),
                  pl.BlockSpec((1, n_heads, dk, dv), lambda i: (i, 0, 0, 0))],
        out_specs=(pl.BlockSpec((1, n_heads, dv), lambda i: (i, 0, 0)),
                   pl.BlockSpec((1, n_heads, dk, dv), lambda i: (i, 0, 0, 0))),
        compiler_params=_params("parallel"),
        name="retention_sample",
    )(qc, kc, v.reshape(b, n_heads, dv), g.reshape(b, n_heads, dv), g1, state)


def _select_topk(s, idx, n_valid, axis):
    big = float(s.shape[axis])
    s = jnp.where(idx < n_valid, s, NEG_INF)
    sel = jnp.zeros(s.shape, F32)
    for j in range(min(MOBA_TOPK, s.shape[axis])):
        m = jnp.max(s, axis=axis, keepdims=True)
        first = jnp.min(jnp.where(s == m, idx, big), axis=axis, keepdims=True)
        pick = idx == first
        sel = jnp.where(pick, jnp.where(n_valid > j, 1.0, 0.0), sel)
        s = jnp.where(pick, -jnp.inf, s)
    return sel


def _moba_prompt_body(nb, scale, q_ref, k_ref, v_ref, o_ref, kmean_scr, kbf_scr, vt_scr, sel_scr, m_scr, l_scr,
                      acc_scr, s_buf, p_buf, a_buf):
    blk = MOBA_BLOCK
    sub = blk // 8
    hd = acc_scr.shape[0]
    qb = pl.program_id(1)

    @pl.when(qb == 0)
    def _():
        for n in range(nb):
            kb = k_ref[n * blk:(n + 1) * blk, :]
            kmean_scr[n:n + 1, :] = jnp.mean(kb, axis=0, keepdims=True)
            kbf_scr[n] = kb.astype(BF16)
            vt_scr[n] = v_ref[n * blk:(n + 1) * blk, :].T.astype(BF16)

    qt_hi, qt_lo = _split_hi_lo(q_ref[...].T)
    km_hi, km_lo = _split_hi_lo(kmean_scr[...])
    s_blk = _dot(km_hi, qt_lo) + _dot(km_lo, qt_hi) + _dot(km_hi, qt_hi)
    bidx = lax.broadcasted_iota(jnp.int32, (nb, blk), 0).astype(F32)
    sel = _select_topk(s_blk, bidx, qb.astype(F32), axis=0)
    for n in range(nb):
        sel_scr[n] = jnp.broadcast_to(sel[n:n + 1, :], (8, blk))

    m_scr[...] = jnp.full(m_scr.shape, NEG_INF, F32)
    l_scr[...] = jnp.zeros_like(l_scr)
    acc_scr[...] = jnp.zeros_like(acc_scr)

    def scores(n):
        return (_dot(kbf_scr[n], qt_hi) * (scale * LOG2_E)).reshape(sub, 8, blk)

    def softmax_update(ss):
        m_old = m_scr[...]
        mx = functools.reduce(jnp.maximum, [jnp.max(s, axis=0) for s in ss])
        m_new = jnp.maximum(m_old, jnp.max(mx, axis=0, keepdims=True))
        alpha = jnp.exp2(m_old - m_new)
        ps = [jnp.exp2(s - m_new[None]) for s in ss]
        psum = functools.reduce(jnp.add, [jnp.sum(p, axis=0) for p in ps])
        l_scr[...] = alpha * l_scr[...] + jnp.sum(psum, axis=0, keepdims=True)
        m_scr[...] = m_new
        return alpha, [p.reshape(blk, blk).astype(BF16) for p in ps]

    def accumulate(alpha, blocks, pbs):
        pv = functools.reduce(jnp.add, [_dot(vt_scr[n], pb) for n, pb in zip(blocks, pbs)])
        acc_scr[...] = (acc_scr[...].reshape(hd // 8, 8, blk) * alpha[None]).reshape(hd, blk) + pv

    kpos = (lax.broadcasted_iota(jnp.int32, (sub, 8, blk), 0) * 8
            + lax.broadcasted_iota(jnp.int32, (sub, 8, blk), 1))
    qpos = lax.broadcasted_iota(jnp.int32, (sub, 8, blk), 2)
    alpha, pbs = softmax_update([jnp.where(kpos <= qpos, scores(qb), NEG_INF)])
    accumulate(alpha, [qb], pbs)

    n_items = (qb + 1) // 2
    s_buf[1] = jnp.full(s_buf.shape[1:], NEG_INF, F32)
    p_buf[0] = jnp.zeros(p_buf.shape[1:], BF16)
    a_buf[0] = jnp.ones(a_buf.shape[1:], F32)

    def step(t, slot):
        prev = 1 - slot
        b1 = jnp.minimum(2 * t, nb - 2)
        for j in range(2):
            row = jnp.where(t < n_items, sel_scr[b1 + j], 0.0)
            s_buf[slot, j] = jnp.where(row[None] > 0.5, scores(b1 + j), NEG_INF)
        alpha, pbs = softmax_update([s_buf[prev, 0], s_buf[prev, 1]])
        a_buf[prev] = alpha
        p_buf[prev, 0] = pbs[0]
        p_buf[prev, 1] = pbs[1]
        b3 = jnp.clip(2 * (t - 2), 0, nb - 2)
        accumulate(a_buf[slot], [b3, b3 + 1], [p_buf[slot, 0], p_buf[slot, 1]])

    def two_steps(u, carry):
        step(2 * u, 0)
        step(2 * u + 1, 1)
        return carry

    lax.fori_loop(0, jnp.where(n_items > 0, (n_items + 3) // 2, 0), two_steps, 0)

    inv_l = 1.0 / l_scr[...]
    o = (acc_scr[...].reshape(hd // 8, 8, blk) * inv_l[None]).reshape(hd, blk)
    o_ref[...] = o.T.astype(o_ref.dtype)


def _moba_prompt(q, k, v, s_len, n_heads, hd):
    assert s_len % MOBA_BLOCK == 0 and s_len >= 2 * MOBA_BLOCK
    nb = s_len // MOBA_BLOCK
    blk = MOBA_BLOCK
    return pl.pallas_call(
        functools.partial(_moba_prompt_body, nb, hd ** -0.5),
        out_shape=jax.ShapeDtypeStruct((s_len, n_heads * hd), BF16),
        grid=(n_heads, nb),
        in_specs=[pl.BlockSpec((blk, hd), lambda h, i: (i, h)),
                  pl.BlockSpec((s_len, hd), lambda h, i: (0, h)),
                  pl.BlockSpec((s_len, hd), lambda h, i: (0, h))],
        out_specs=pl.BlockSpec((blk, hd), lambda h, i: (i, h)),
        scratch_shapes=[pltpu.VMEM((nb, hd), F32),
                        pltpu.VMEM((nb, blk, hd), BF16),
                        pltpu.VMEM((nb, hd, blk), BF16),
                        pltpu.VMEM((nb, 8, blk), F32),
                        pltpu.VMEM((8, blk), F32),
                        pltpu.VMEM((8, blk), F32),
                        pltpu.VMEM((hd, blk), F32),
                        pltpu.VMEM((2, 2, blk // 8, 8, blk), F32),
                        pltpu.VMEM((2, 2, blk, blk), BF16),
                        pltpu.VMEM((2, 8, blk), F32)],
        compiler_params=_params("parallel", "arbitrary"),
        name="moba_prompt",
    )(q, k, v)


def _class_reduce(x, op, period):
    w = x.shape[1]
    y = functools.reduce(op, [x[:, i * LANES:(i + 1) * LANES] for i in range(w // LANES)])
    y = jnp.broadcast_to(y, (8, LANES))
    shift = LANES // 2
    while shift >= period:
        y = op(y, pltpu.roll(y, shift, 1))
        shift //= 2
    return jnp.concatenate([y[0:1]] * (w // LANES), axis=1)


def _select_topk_lanes(s, n_valid, period):
    w = s.shape[1]
    blk = (lax.broadcasted_iota(jnp.int32, (1, w), 1) // period).astype(F32)
    s = jnp.where(blk < n_valid, s, NEG_INF)
    sel = jnp.zeros((1, w), F32)
    for j in range(min(MOBA_TOPK, w // period)):
        m = _class_reduce(s, jnp.maximum, period)
        first = _class_reduce(jnp.where(s == m, blk, float(w)), jnp.minimum, period)
        pick = blk == first
        sel = jnp.where(pick, 1.0 if j < n_valid else 0.0, sel)
        s = jnp.where(pick, -jnp.inf, s)
    return sel


def _moba_sample_body(G, NG, page, scale, pt_ref, *refs):
    ck = refs[:G]
    cv = refs[G:2 * G]
    q_ref, kn_ref, vn_ref, diag_ref, o_ref, s_all, psum, p_all, acc, pnew = refs[2 * G:]
    H, hd = q_ref.shape[1:]
    W = page * H
    NP = G * NG
    ppb = MOBA_BLOCK // page
    cb = NP // ppb
    t = pl.program_id(1)

    def diag_rows(res):
        return jnp.sum(res * diag_ref[:, :res.shape[1]], axis=0, keepdims=True)

    def pad_rows(x, rows):
        return jnp.concatenate([x, jnp.zeros((rows - x.shape[0], x.shape[1]), x.dtype)], axis=0)

    @pl.when(t < NG)
    def _k_phase():
        q_bf = q_ref[0].astype(BF16)
        for g in range(G):
            kp = ck[g][0]
            psum[t * G + g] = jnp.sum(kp.reshape(page, H, hd), axis=0)
            s_all[t, g:g + 1, :] = diag_rows(_dot_nt(q_bf, kp.astype(BF16))) * scale

    @pl.when(t == NG - 1)
    def _select_and_softmax():
        q_hi, q_lo = _split_hi_lo(q_ref[0])
        kn = kn_ref[0]
        kmean = [sum(psum[n * ppb + j] for j in range(ppb)) / MOBA_BLOCK for n in range(cb)] + [kn / MOBA_BLOCK]
        wb = -(-(cb + 1) * H // LANES) * LANES
        km_hi, km_lo = _split_hi_lo(pad_rows(jnp.concatenate(kmean, axis=0), wb))
        s_blk = diag_rows(_dot_nt(q_lo, km_hi) + _dot_nt(q_hi, km_lo) + _dot_nt(q_hi, km_hi))
        sel = _select_topk_lanes(s_blk, cb, H)
        lane_blk = lax.broadcasted_iota(jnp.int32, (1, wb), 1) // H
        s_new = _class_reduce(diag_rows(_dot_nt(q_hi, pad_rows(kn, LANES).astype(BF16))), jnp.add, H) * scale
        s_new = jnp.concatenate([s_new] * (W // LANES), axis=1)
        m = None
        for pg in range(NP):
            chosen = _class_reduce(jnp.where(lane_blk == pg // ppb, sel, 0.0), jnp.add, H)[:, :LANES]
            chosen = jnp.concatenate([chosen] * (W // LANES), axis=1)
            sm = jnp.where(chosen > 0.5, s_all[pg // G, pg % G:pg % G + 1, :], NEG_INF)
            s_all[pg // G, pg % G:pg % G + 1, :] = sm
            m = sm if m is None else jnp.maximum(m, sm)
        m = jnp.maximum(_class_reduce(m, jnp.maximum, H), s_new)
        l = None
        for pg in range(NP):
            p = jnp.exp(s_all[pg // G, pg % G:pg % G + 1, :] - m)
            s_all[pg // G, pg % G:pg % G + 1, :] = p
            l = p if l is None else l + p
        p_new = jnp.exp(s_new - m)
        inv = 1.0 / (_class_reduce(l, jnp.add, H) + p_new)
        for pg in range(NP):
            p_all[pg // G, pg % G:pg % G + 1, :] = s_all[pg // G, pg % G:pg % G + 1, :] * inv
        pnew[...] = (p_new * inv)[:, :LANES]

    @pl.when(t == NG)
    def _():
        acc[...] = jnp.zeros_like(acc)

    @pl.when(t >= NG)
    def _v_phase():
        for g in range(G):
            pm = (jnp.broadcast_to(p_all[t - NG, g:g + 1, :], (H, W)) * diag_ref[...]).astype(BF16)
            acc[...] += _dot(pm, cv[g][0].astype(BF16))

    @pl.when(t == 2 * NG - 1)
    def _():
        pm = (jnp.broadcast_to(pnew[...], (H, LANES)) * diag_ref[:, :LANES]).astype(BF16)
        o_ref[0] = acc[...] + _dot(pm, pad_rows(vn_ref[0], LANES).astype(BF16))


def _moba_sample(q, kn, vn, cache_k, cache_v, page_table):
    n_pool, page, n_heads, hd = cache_k.shape
    b, n_pages = page_table.shape
    da = n_heads * hd
    assert MOBA_BLOCK % page == 0 and n_pages % (MOBA_BLOCK // page) == 0
    assert n_heads % 8 == 0 and LANES % n_heads == 0 and n_heads & (n_heads - 1) == 0 and hd == LANES
    G = 4 if n_pages % 4 == 0 else (2 if n_pages % 2 == 0 else 1)
    NG = n_pages // G
    w = page * n_heads
    diag = (jnp.arange(w)[None, :] % n_heads == jnp.arange(n_heads)[:, None]).astype(F32)

    def k_map(g):
        return lambda i, t, pt: (pt[i, jnp.minimum(t, NG - 1) * G + g], 0, 0)

    def v_map(g):
        return lambda i, t, pt: (pt[i, jnp.maximum(t - NG, 0) * G + g], 0, 0)

    ck = cache_k.reshape(n_pool, w, hd)
    cv = cache_v.reshape(n_pool, w, hd)
    per_b = lambda i, t, pt: (i, 0, 0)
    grid_spec = pltpu.PrefetchScalarGridSpec(
        num_scalar_prefetch=1,
        grid=(b, 2 * NG),
        in_specs=[pl.BlockSpec((1, w, hd), k_map(g)) for g in range(G)]
        + [pl.BlockSpec((1, w, hd), v_map(g)) for g in range(G)]
        + [pl.BlockSpec((1, n_heads, hd), per_b)] * 3
        + [pl.BlockSpec((n_heads, w), lambda i, t, pt: (0, 0))],
        out_specs=pl.BlockSpec((1, n_heads, hd), per_b),
        scratch_shapes=[pltpu.VMEM((NG, G, w), F32),
                        pltpu.VMEM((n_pages, n_heads, hd), F32),
                        pltpu.VMEM((NG, G, w), F32),
                        pltpu.VMEM((n_heads, hd), F32),
                        pltpu.VMEM((1, LANES), F32)])
    out = pl.pallas_call(
        functools.partial(_moba_sample_body, G, NG, page, hd ** -0.5),
        out_shape=jax.ShapeDtypeStruct((b, n_heads, hd), F32),
        grid_spec=grid_spec,
        compiler_params=_params("parallel", "arbitrary"),
        name="moba_sample",
    )(page_table, *([ck] * G), *([cv] * G), q.reshape(b, n_heads, hd), kn.reshape(b, n_heads, hd),
      vn.reshape(b, n_heads, hd), diag)
    return out.reshape(b, da)


def _rope_tables(pos, d_head):
    inv = ROPE_THETA ** (-jnp.arange(0, d_head, 2, dtype=F32) / d_head)
    ang = pos.astype(F32)[:, None] * inv[None, :]
    cos, sin = jnp.cos(ang), jnp.sin(ang)
    if d_head // 2 == LANES:
        return cos, sin
    assert d_head == LANES
    return jnp.concatenate([cos, cos], axis=1), jnp.concatenate([-sin, sin], axis=1)


def _ffn(h, g, w_in, w_out):
    a = _norm_matmul(h, g, w_in.astype(BF16), BF16, mode="swiglu")
    return _matmul_residual(a, w_out.astype(BF16), h, 0.5)


def kernel(x_prompt, x_sample, state_ret, cache_k, cache_v, page_table, norm_ffa, ffa_w_in, ffa_w_out, norm_mix, norm_ffb, ffb_w_in, ffb_w_out, ret_w_in, ret_w_out, kv_norm, w_kv, moba_w_q, moba_w_o, final_norm):
    bp, s_len, d = x_prompt.shape
    db, dl, _ = x_sample.shape
    n_a, _, ret_heads, dk, dv = state_ret.shape
    _, page, att_heads, hd = cache_k.shape
    assert bp == 1 and dl == 1 and n_a == 1 and norm_ffa.shape[0] == 2 and s_len % db == 0
    assert s_len % RET_CHUNK == 0 and s_len % MOBA_BLOCK == 0
    past_len = page_table.shape[1] * page
    qk_w, v_w, att_w = ret_heads * dk, ret_heads * dv, att_heads * hd

    pos = jnp.concatenate([jnp.arange(s_len, dtype=jnp.int32), jnp.full((db,), past_len, jnp.int32)])
    cos_r, sin_r = _rope_tables(pos, dk)
    cos_m, sin_m = _rope_tables(pos, hd)
    log_g = jnp.log(1.0 - 2.0 ** (-5.0 - jnp.arange(ret_heads, dtype=F32)))

    h = jnp.concatenate([x_prompt[0], x_sample[:, 0]], axis=0)

    h = _ffn(h, norm_ffa[0], ffa_w_in[0], ffa_w_out[0])
    w_ret = ret_w_in[0]
    q = _norm_matmul(h, norm_mix[0], w_ret[:, :qk_w].astype(BF16), F32, "rope", cos_r, sin_r, dk)
    k = _norm_matmul(h, norm_mix[0], w_ret[:, qk_w:2 * qk_w].astype(BF16), F32, "rope", cos_r, sin_r, dk, dk ** -0.5)
    vg = _norm_matmul(h, norm_mix[0], w_ret[:, 2 * qk_w:].astype(BF16), BF16)
    o_p, st_p = _retention_prompt(q, k, vg, s_len, ret_heads, dk, dv, log_g)
    vg_s = vg[s_len:].astype(F32)
    o_s, st_s = _retention_sample(q[s_len:], k[s_len:], vg_s[:, :v_w], vg_s[:, v_w:], state_ret[0], log_g)
    o = jnp.concatenate([o_p, o_s.reshape(db, v_w).astype(BF16)], axis=0)
    h = _matmul_residual(o, ret_w_out[0].astype(BF16), h, 1.0)
    h = _ffn(h, norm_ffb[0], ffb_w_in[0], ffb_w_out[0])

    k_sh = _norm_matmul(h, kv_norm, w_kv[:, :att_w].astype(BF16), F32, "rope", cos_m, sin_m, hd)
    v_sh = _norm_matmul(h, kv_norm, w_kv[:, att_w:].astype(BF16), F32)

    h = _ffn(h, norm_ffa[1], ffa_w_in[1], ffa_w_out[1])
    q = _norm_matmul(h, norm_mix[1], moba_w_q[0].astype(BF16), F32, "rope", cos_m, sin_m, hd)
    a_p = _moba_prompt(q, k_sh, v_sh, s_len, att_heads, hd)
    a_s = _moba_sample(q[s_len:], k_sh[s_len:], v_sh[s_len:], cache_k, cache_v, page_table)
    a = jnp.concatenate([a_p, a_s.astype(BF16)], axis=0)
    h = _matmul_residual(a, moba_w_o[0].astype(BF16), h, 1.0)
    h = _ffn(h, norm_ffb[1], ffb_w_in[1], ffb_w_out[1])
    y = _rmsnorm(h, final_norm)

    return (y[:s_len].reshape(1, s_len, d),
            y[s_len:].reshape(db, 1, d),
            st_p[None, None],
            st_s[None],
            k_sh[:s_len].reshape(1, s_len, att_heads, hd),
            v_sh[:s_len].reshape(1, s_len, att_heads, hd),
            k_sh[s_len:].reshape(db, 1, att_heads, hd),
            v_sh[s_len:].reshape(db, 1, att_heads, hd))
```

```python
import functools

import jax
import jax.numpy as jnp
from jax import lax
from jax.experimental import pallas as pl
from jax.experimental.pallas import tpu as pltpu

F32 = jnp.float32
BF16 = jnp.bfloat16

NORM_EPS = 1e-6
GN_EPS = 1e-5
NEG_INF = -1e30
ROPE_THETA = 10000.0
MOBA_BLOCK = 256
MOBA_TOPK = 3
RET_CHUNK = 128
LOG2_E = 1.4426950408889634
LANES = 128
VMEM_LIMIT = 56 * 1024 * 1024


def _params(*sem):
    return pltpu.CompilerParams(dimension_semantics=sem, vmem_limit_bytes=VMEM_LIMIT)


def _dot(a, b):
    return jnp.dot(a, b, preferred_element_type=F32)


def _dot_nt(a, b):
    return lax.dot_general(a, b, (((1,), (1,)), ((), ())), preferred_element_type=F32)


def _dot_tn(a, b):
    return lax.dot_general(a, b, (((0,), (0,)), ((), ())), preferred_element_type=F32)


def _split_hi_lo(x):
    hi = x.astype(BF16)
    lo = (x - hi.astype(F32)).astype(BF16)
    return hi, lo


def _silu(x):
    return x * jax.nn.sigmoid(x)


def _row_tile(m):
    for t in (1040, 1024, 640, 512, 256, 128, 64, 32, 16):
        if m % t == 0:
            return t
    raise ValueError(f"row count {m} has no supported tile")


def _col_tile(n, cap=1024):
    for t in (1024, 512, 256, 128):
        if t > cap:
            continue
        if n % t == 0:
            return t
    raise ValueError(f"column count {n} has no supported tile")


def _rms(x, g):
    return x * lax.rsqrt(jnp.mean(x * x, axis=-1, keepdims=True) + NORM_EPS) * g


def _rope_store(o_ref, acc, cos, sin, d_head, scale):
    tn = acc.shape[1]
    half = d_head // 2
    for h in range(tn // d_head):
        lo = h * d_head
        if half % LANES == 0:
            x1 = acc[:, lo:lo + half]
            x2 = acc[:, lo + half:lo + d_head]
            r1 = x1 * cos - x2 * sin
            r2 = x2 * cos + x1 * sin
            if scale != 1.0:
                r1, r2 = r1 * scale, r2 * scale
            o_ref[:, lo:lo + half] = r1.astype(o_ref.dtype)
            o_ref[:, lo + half:lo + d_head] = r2.astype(o_ref.dtype)
        else:
            xh = acc[:, lo:lo + d_head]
            r = xh * cos + pltpu.roll(xh, half, 1) * sin
            if scale != 1.0:
                r = r * scale
            o_ref[:, lo:lo + d_head] = r.astype(o_ref.dtype)


def _norm_mm_body(mode, d_head, scale, x_ref, g_ref, *refs):
    xn_ref = refs[-1]

    @pl.when(pl.program_id(1) == 0)
    def _():
        xn_ref[...] = _rms(x_ref[...], g_ref[...]).astype(BF16)

    xn = xn_ref[...]
    if mode == "swiglu":
        wg_ref, wu_ref, o_ref = refs[:3]
        gate = _dot(xn, wg_ref[...])
        up = _dot(xn, wu_ref[...])
        o_ref[...] = (_silu(gate) * up).astype(o_ref.dtype)
    elif mode == "rope":
        w_ref, cos_ref, sin_ref, o_ref = refs[:4]
        _rope_store(o_ref, _dot(xn, w_ref[...]), cos_ref[...], sin_ref[...], d_head, scale)
    else:
        w_ref, o_ref = refs[:2]
        o_ref[...] = _dot(xn, w_ref[...]).astype(o_ref.dtype)


def _norm_matmul(x, g, w, out_dtype, mode="plain", cos=None, sin=None, d_head=0, scale=1.0, rows=None):
    d = x.shape[1]
    first, m = rows if rows is not None else (0, x.shape[0])
    n = w.shape[1] // 2 if mode == "swiglu" else w.shape[1]
    tm, tn = _row_tile(m), _col_tile(n)
    assert first % tm == 0
    i0 = first // tm
    nj = n // tn
    in_specs = [pl.BlockSpec((tm, d), lambda i, j: (i + i0, 0)), pl.BlockSpec((1, d), lambda i, j: (0, 0))]
    args = [x, g.reshape(1, d)]
    if mode == "swiglu":
        in_specs += [pl.BlockSpec((d, tn), lambda i, j: (0, j)), pl.BlockSpec((d, tn), lambda i, j: (0, j + nj))]
        args += [w, w]
    else:
        in_specs.append(pl.BlockSpec((d, tn), lambda i, j: (0, j)))
        args.append(w)
    if mode == "rope":
        assert tn % d_head == 0
        in_specs += [pl.BlockSpec((tm, LANES), lambda i, j: (i + i0, 0))] * 2
        args += [cos, sin]
    return pl.pallas_call(
        functools.partial(_norm_mm_body, mode, d_head, scale),
        out_shape=jax.ShapeDtypeStruct((m, n), out_dtype),
        grid=(m // tm, nj),
        in_specs=in_specs,
        out_specs=pl.BlockSpec((tm, tn), lambda i, j: (i, j)),
        scratch_shapes=[pltpu.VMEM((tm, d), BF16)],
        compiler_params=_params("parallel", "arbitrary"),
        name=f"norm_matmul_{mode}",
    )(*args)


def _mm_res_body(scale, a_ref, w_ref, h_ref, o_ref):
    acc = _dot(a_ref[...], w_ref[...])
    o_ref[...] = h_ref[...] + (acc if scale == 1.0 else scale * acc)


def _matmul_residual(a, w, h, scale):
    m, k = a.shape
    n = w.shape[1]
    tm, tn = _row_tile(m), _col_tile(n, cap=512)
    return pl.pallas_call(
        functools.partial(_mm_res_body, scale),
        out_shape=jax.ShapeDtypeStruct((m, n), F32),
        grid=(m // tm, n // tn),
        in_specs=[pl.BlockSpec((tm, k), lambda i, j: (i, 0)),
                  pl.BlockSpec((k, tn), lambda i, j: (0, j)),
                  pl.BlockSpec((tm, tn), lambda i, j: (i, j))],
        out_specs=pl.BlockSpec((tm, tn), lambda i, j: (i, j)),
        compiler_params=_params("parallel", "arbitrary"),
        name="matmul_residual",
    )(a, w, h)


def _rmsnorm_body(x_ref, g_ref, o_ref):
    o_ref[...] = _rms(x_ref[...], g_ref[...])


def _rmsnorm(x, g, rows):
    d = x.shape[1]
    first, m = rows
    tm = _row_tile(m)
    assert first % tm == 0
    i0 = first // tm
    return pl.pallas_call(
        _rmsnorm_body,
        out_shape=jax.ShapeDtypeStruct((m, d), F32),
        grid=(m // tm,),
        in_specs=[pl.BlockSpec((tm, d), lambda i: (i + i0, 0)), pl.BlockSpec((1, d), lambda i: (0, 0))],
        out_specs=pl.BlockSpec((tm, d), lambda i: (i, 0)),
        compiler_params=_params("parallel"),
        name="final_rmsnorm",
    )(x, g.reshape(1, d))


def _group_norm_gate(o, g):
    mu = jnp.mean(o, axis=-1, keepdims=True)
    d = o - mu
    var = jnp.mean(d * d, axis=-1, keepdims=True)
    return d * lax.rsqrt(var + GN_EPS) * _silu(g)


def _ret_prompt_body(q_ref, k_ref, v_ref, g_ref, dmat_ref, cdec_ref, kdec_ref, gl_ref, o_ref, sfin_ref, s_scr):
    c = pl.program_id(1)

    @pl.when(c == 0)
    def _():
        s_scr[...] = jnp.zeros_like(s_scr)

    L = dmat_ref.shape[1]
    s = s_scr[...]
    for u in range(q_ref.shape[0] // L):
        rows = slice(u * L, (u + 1) * L)
        q = q_ref[rows, :].astype(BF16)
        k = k_ref[rows, :]
        v = v_ref[rows, :]
        a = _dot_nt(q, k.astype(BF16)) * dmat_ref[0]
        inner = _dot(a.astype(BF16), v)
        cross = _dot(q, s.astype(BF16)) * cdec_ref[0]
        kd = (k * kdec_ref[0]).astype(BF16)
        s = gl_ref[0] * s + _dot_tn(kd, v)
        o_ref[rows, :] = _group_norm_gate(inner + cross, g_ref[rows, :].astype(F32)).astype(o_ref.dtype)
    s_scr[...] = s

    @pl.when(c == pl.num_programs(1) - 1)
    def _():
        sfin_ref[0] = s


def _retention_prompt(q, k, vg, s_len, n_heads, dk, dv, log_g):
    L = RET_CHUNK
    per_step = 4 if s_len % (4 * L) == 0 else 1
    T = per_step * L
    i = jnp.arange(L, dtype=F32)
    diff = i[:, None] - i[None, :]
    dmat = jnp.where(diff >= 0, jnp.exp(log_g[:, None, None] * jnp.maximum(diff, 0.0)), 0.0)
    cdec = jnp.exp(log_g[:, None] * (i[None, :] + 1.0))[:, :, None]
    kdec = jnp.exp(log_g[:, None] * (L - 1.0 - i)[None, :])[:, :, None]
    gl = jnp.broadcast_to(jnp.exp(log_g * L)[:, None, None], (n_heads, 1, dv))
    return pl.pallas_call(
        _ret_prompt_body,
        out_shape=(jax.ShapeDtypeStruct((s_len, n_heads * dv), BF16),
                   jax.ShapeDtypeStruct((n_heads, dk, dv), F32)),
        grid=(n_heads, s_len // T),
        in_specs=[pl.BlockSpec((T, dk), lambda h, c: (c, h)),
                  pl.BlockSpec((T, dk), lambda h, c: (c, h)),
                  pl.BlockSpec((T, dv), lambda h, c: (c, h)),
                  pl.BlockSpec((T, dv), lambda h, c: (c, h + n_heads)),
                  pl.BlockSpec((1, L, L), lambda h, c: (h, 0, 0)),
                  pl.BlockSpec((1, L, 1), lambda h, c: (h, 0, 0)),
                  pl.BlockSpec((1, L, 1), lambda h, c: (h, 0, 0)),
                  pl.BlockSpec((1, 1, dv), lambda h, c: (h, 0, 0))],
        out_specs=(pl.BlockSpec((T, dv), lambda h, c: (c, h)),
                   pl.BlockSpec((1, dk, dv), lambda h, c: (h, 0, 0))),
        scratch_shapes=[pltpu.VMEM((dk, dv), F32)],
        compiler_params=_params("parallel", "arbitrary"),
        name="retention_prompt",
    )(q, k, vg, vg, dmat, cdec, kdec, gl)


def _ret_sample_body(n_heads, qc_ref, kc_ref, v_ref, g_ref, g1_ref, s_ref, o_ref, snew_ref):
    qc = qc_ref[0]
    kc = kc_ref[0]
    for h in range(n_heads):
        s = s_ref[0, h]
        qh = qc[:, h:h + 1]
        kh = kc[:, h:h + 1]
        vh = v_ref[0, h:h + 1, :]
        g1 = g1_ref[h:h + 1, :]
        a = jnp.sum(qh * kh, axis=0, keepdims=True)
        cross = jnp.sum(s * qh, axis=0, keepdims=True) * g1
        snew_ref[0, h] = g1 * s + kh * vh
        o = a * vh + cross
        o_ref[0, h:h + 1, :] = _group_norm_gate(o, g_ref[0, h:h + 1, :])


def _retention_sample(q, k, v, g, state, log_g):
    b, n_heads, dk, dv = state.shape
    qc = q.reshape(b, n_heads, dk).transpose(0, 2, 1)
    kc = k.reshape(b, n_heads, dk).transpose(0, 2, 1)
    g1 = jnp.broadcast_to(jnp.exp(log_g * 1.0)[:, None], (n_heads, dv))
    return pl.pallas_call(
        functools.partial(_ret_sample_body, n_heads),
        out_shape=(jax.ShapeDtypeStruct((b, n_heads, dv), F32),
                   jax.ShapeDtypeStruct((b, n_heads, dk, dv), F32)),
        grid=(b,),
        in_specs=[pl.BlockSpec((1, dk, n_heads), lambda i: (i, 0, 0)),
                  pl.BlockSpec((1, dk, n_heads), lambda i: (i, 0, 0)),
                  pl.BlockSpec((1, n_heads, dv), lambda i: (i, 0, 0)),
                  pl.BlockSpec((1, n_heads, dv), lambda i: (i, 0, 0)),
                  pl.BlockSpec((n_heads, dv), lambda i: (0, 0)),
                  pl.BlockSpec((1, n_heads, dk, dv), lambda i: (i, 0, 0, 0))],
        out_specs=(pl.BlockSpec((1, n_heads, dv), lambda i: (i, 0, 0)),
                   pl.BlockSpec((1, n_heads, dk, dv), lambda i: (i, 0, 0, 0))),
        compiler_params=_params("parallel"),
        name="retention_sample",
    )(qc, kc, v.reshape(b, n_heads, dv), g.reshape(b, n_heads, dv), g1, state)


def _select_topk(s, idx, n_valid, axis):
    big = float(s.shape[axis])
    s = jnp.where(idx < n_valid, s, NEG_INF)
    sel = jnp.zeros(s.shape, F32)
    for j in range(min(MOBA_TOPK, s.shape[axis])):
        m = jnp.max(s, axis=axis, keepdims=True)
        first = jnp.min(jnp.where(s == m, idx, big), axis=axis, keepdims=True)
        pick = idx == first
        sel = jnp.where(pick, jnp.where(n_valid > j, 1.0, 0.0), sel)
        s = jnp.where(pick, -jnp.inf, s)
    return sel


def _moba_prompt_body(nb, scale, q_ref, k_ref, v_ref, o_ref, kmean_scr, kbf_scr, vt_scr, sel_scr, m_scr, l_scr,
                      acc_scr, s_buf, p_buf, a_buf):
    blk = MOBA_BLOCK
    sub = blk // 8
    hd = acc_scr.shape[0]
    qb = pl.program_id(1)

    @pl.when(qb == 0)
    def _():
        for n in range(nb):
            kb = k_ref[n * blk:(n + 1) * blk, :]
            kmean_scr[n:n + 1, :] = jnp.mean(kb, axis=0, keepdims=True)
            kbf_scr[n] = kb.astype(BF16)
            vt_scr[n] = v_ref[n * blk:(n + 1) * blk, :].T.astype(BF16)

    qt = q_ref[...].T
    qt_hi, qt_lo = _split_hi_lo(qt)
    qt_scaled = (qt * (scale * LOG2_E)).astype(BF16)
    km_hi, km_lo = _split_hi_lo(kmean_scr[...])
    s_blk = _dot(km_hi, qt_lo) + _dot(km_lo, qt_hi) + _dot(km_hi, qt_hi)
    bidx = lax.broadcasted_iota(jnp.int32, (nb, blk), 0).astype(F32)
    sel = _select_topk(s_blk, bidx, qb.astype(F32), axis=0)
    for n in range(nb):
        sel_scr[n] = jnp.broadcast_to(sel[n:n + 1, :], (8, blk))

    m_scr[...] = jnp.full(m_scr.shape, NEG_INF, F32)
    l_scr[...] = jnp.zeros_like(l_scr)
    acc_scr[...] = jnp.zeros_like(acc_scr)

    def scores(n):
        return _dot(kbf_scr[n], qt_scaled).reshape(sub, 8, blk)

    def softmax_update(ss):
        m_old = m_scr[...]
        mx = functools.reduce(jnp.maximum, [jnp.max(s, axis=0) for s in ss])
        m_new = jnp.maximum(m_old, jnp.max(mx, axis=0, keepdims=True))
        alpha = jnp.exp2(m_old - m_new)
        ps = [jnp.exp2(s - m_new[None]) for s in ss]
        psum = functools.reduce(jnp.add, [jnp.sum(p, axis=0) for p in ps])
        l_scr[...] = alpha * l_scr[...] + jnp.sum(psum, axis=0, keepdims=True)
        m_scr[...] = m_new
        return alpha, [p.reshape(blk, blk).astype(BF16) for p in ps]

    def accumulate(alpha, blocks, pbs):
        pv = functools.reduce(jnp.add, [_dot(vt_scr[n], pb) for n, pb in zip(blocks, pbs)])
        acc_scr[...] = (acc_scr[...].reshape(hd // 8, 8, blk) * alpha[None]).reshape(hd, blk) + pv

    kpos = (lax.broadcasted_iota(jnp.int32, (sub, 8, blk), 0) * 8
            + lax.broadcasted_iota(jnp.int32, (sub, 8, blk), 1))
    qpos = lax.broadcasted_iota(jnp.int32, (sub, 8, blk), 2)
    alpha, pbs = softmax_update([jnp.where(kpos <= qpos, scores(qb), NEG_INF)])
    accumulate(alpha, [qb], pbs)

    n_items = (qb + 1) // 2
    s_buf[1] = jnp.full(s_buf.shape[1:], NEG_INF, F32)
    p_buf[0] = jnp.zeros(p_buf.shape[1:], BF16)
    a_buf[0] = jnp.ones(a_buf.shape[1:], F32)

    def step(t, slot):
        prev = 1 - slot
        b1 = jnp.minimum(2 * t, nb - 2)
        for j in range(2):
            row = jnp.where(t < n_items, sel_scr[b1 + j], 0.0)
            s_buf[slot, j] = jnp.where(row[None] > 0.5, scores(b1 + j), NEG_INF)
        alpha, pbs = softmax_update([s_buf[prev, 0], s_buf[prev, 1]])
        a_buf[prev] = alpha
        p_buf[prev, 0] = pbs[0]
        p_buf[prev, 1] = pbs[1]
        b3 = jnp.clip(2 * (t - 2), 0, nb - 2)
        accumulate(a_buf[slot], [b3, b3 + 1], [p_buf[slot, 0], p_buf[slot, 1]])

    def two_steps(u, carry):
        step(2 * u, 0)
        step(2 * u + 1, 1)
        return carry

    lax.fori_loop(0, jnp.where(n_items > 0, (n_items + 3) // 2, 0), two_steps, 0)

    inv_l = 1.0 / l_scr[...]
    o = (acc_scr[...].reshape(hd // 8, 8, blk) * inv_l[None]).reshape(hd, blk)
    o_ref[...] = o.T.astype(o_ref.dtype)


def _moba_prompt(q, k, v, s_len, n_heads, hd):
    assert s_len % MOBA_BLOCK == 0 and s_len >= 2 * MOBA_BLOCK
    nb = s_len // MOBA_BLOCK
    blk = MOBA_BLOCK
    return pl.pallas_call(
        functools.partial(_moba_prompt_body, nb, hd ** -0.5),
        out_shape=jax.ShapeDtypeStruct((s_len, n_heads * hd), BF16),
        grid=(n_heads, nb),
        in_specs=[pl.BlockSpec((blk, hd), lambda h, i: (i, h)),
                  pl.BlockSpec((s_len, hd), lambda h, i: (0, h)),
                  pl.BlockSpec((s_len, hd), lambda h, i: (0, h))],
        out_specs=pl.BlockSpec((blk, hd), lambda h, i: (i, h)),
        scratch_shapes=[pltpu.VMEM((nb, hd), F32),
                        pltpu.VMEM((nb, blk, hd), BF16),
                        pltpu.VMEM((nb, hd, blk), BF16),
                        pltpu.VMEM((nb, 8, blk), F32),
                        pltpu.VMEM((8, blk), F32),
                        pltpu.VMEM((8, blk), F32),
                        pltpu.VMEM((hd, blk), F32),
                        pltpu.VMEM((2, 2, blk // 8, 8, blk), F32),
                        pltpu.VMEM((2, 2, blk, blk), BF16),
                        pltpu.VMEM((2, 8, blk), F32)],
        compiler_params=_params("parallel", "arbitrary"),
        name="moba_prompt",
    )(q, k, v)


def _class_reduce(x, op, period):
    w = x.shape[1]
    y = functools.reduce(op, [x[:, i * LANES:(i + 1) * LANES] for i in range(w // LANES)])
    y = jnp.broadcast_to(y, (8, LANES))
    shift = LANES // 2
    while shift >= period:
        y = op(y, pltpu.roll(y, shift, 1))
        shift //= 2
    return jnp.concatenate([y[0:1]] * (w // LANES), axis=1)


def _block_rows(s, n_blocks, period):
    per_tile = LANES // period
    rows = []
    for n in range(n_blocks):
        tile = jnp.broadcast_to(s[:, (n // per_tile) * LANES:(n // per_tile + 1) * LANES], (8, LANES))
        shift = (LANES - (n % per_tile) * period) % LANES
        rows.append((pltpu.roll(tile, shift, 1) if shift else tile)[0:1])
    return jnp.concatenate(rows, axis=0)


def _replicate_classes(x, period):
    lane = lax.broadcasted_iota(jnp.int32, x.shape, 1)
    x = jnp.where(lane < period, x, 0.0)
    shift = period
    while shift < LANES:
        x = x + pltpu.roll(x, shift, 1)
        shift *= 2
    return x


def _moba_sample_body(G, NG, page, scale, pt_ref, *refs):
    ck = refs[:G]
    cv = refs[G:2 * G]
    q_ref, kn_ref, vn_ref, diag_ref, o_ref, s_all, psum, p_all, acc, pnew = refs[2 * G:]
    H, hd = q_ref.shape[1:]
    W = page * H
    NP = G * NG
    ppb = MOBA_BLOCK // page
    cb = NP // ppb
    t = pl.program_id(1)

    def diag_rows(res):
        return jnp.sum(res * diag_ref[:, :res.shape[1]], axis=0, keepdims=True)

    def pad_rows(x, rows):
        return jnp.concatenate([x, jnp.zeros((rows - x.shape[0], x.shape[1]), x.dtype)], axis=0)

    @pl.when(t < NG)
    def _k_phase():
        q_bf = q_ref[0].astype(BF16)
        for g in range(G):
            kp = ck[g][0]
            psum[t * G + g] = jnp.sum(kp.reshape(page, H, hd), axis=0)
            s_all[t, g:g + 1, :] = diag_rows(_dot_nt(q_bf, kp.astype(BF16))) * scale

    @pl.when(t == NG - 1)
    def _select_and_softmax():
        q_hi, q_lo = _split_hi_lo(q_ref[0])
        kn = kn_ref[0]
        kmean = [sum(psum[n * ppb + j] for j in range(ppb)) / MOBA_BLOCK for n in range(cb)] + [kn / MOBA_BLOCK]
        wb = -(-(cb + 1) * H // LANES) * LANES
        km_hi, km_lo = _split_hi_lo(pad_rows(jnp.concatenate(kmean, axis=0), wb))
        s_blk = diag_rows(_dot_nt(q_lo, km_hi) + _dot_nt(q_hi, km_lo) + _dot_nt(q_hi, km_hi))
        s_rows = pad_rows(_block_rows(s_blk, cb + 1, H), -(-(cb + 1) // 8) * 8)
        bidx = lax.broadcasted_iota(jnp.int32, s_rows.shape, 0).astype(F32)
        sel = _replicate_classes(_select_topk(s_rows, bidx, float(cb), axis=0), H)
        s_new = _class_reduce(diag_rows(_dot_nt(q_hi, pad_rows(kn, LANES).astype(BF16))), jnp.add, H) * scale
        s_new = jnp.concatenate([s_new] * (W // LANES), axis=1)
        m = None
        for pg in range(NP):
            chosen = jnp.concatenate([sel[pg // ppb:pg // ppb + 1, :]] * (W // LANES), axis=1)
            sm = jnp.where(chosen > 0.5, s_all[pg // G, pg % G:pg % G + 1, :], NEG_INF)
            s_all[pg // G, pg % G:pg % G + 1, :] = sm
            m = sm if m is None else jnp.maximum(m, sm)
        m = jnp.maximum(_class_reduce(m, jnp.maximum, H), s_new)
        l = None
        for pg in range(NP):
            p = jnp.exp(s_all[pg // G, pg % G:pg % G + 1, :] - m)
            s_all[pg // G, pg % G:pg % G + 1, :] = p
            l = p if l is None else l + p
        p_new = jnp.exp(s_new - m)
        inv = 1.0 / (_class_reduce(l, jnp.add, H) + p_new)
        for pg in range(NP):
            p_all[pg // G, pg % G:pg % G + 1, :] = s_all[pg // G, pg % G:pg % G + 1, :] * inv
        pnew[...] = (p_new * inv)[:, :LANES]

    @pl.when(t == NG)
    def _():
        acc[...] = jnp.zeros_like(acc)

    @pl.when(t >= NG)
    def _v_phase():
        for g in range(G):
            pm = (jnp.broadcast_to(p_all[t - NG, g:g + 1, :], (H, W)) * diag_ref[...]).astype(BF16)
            acc[...] += _dot(pm, cv[g][0].astype(BF16))

    @pl.when(t == 2 * NG - 1)
    def _():
        pm = (jnp.broadcast_to(pnew[...], (H, LANES)) * diag_ref[:, :LANES]).astype(BF16)
        o_ref[0] = acc[...] + _dot(pm, pad_rows(vn_ref[0], LANES).astype(BF16))


def _moba_sample(q, kn, vn, cache_k, cache_v, page_table):
    n_pool, page, n_heads, hd = cache_k.shape
    b, n_pages = page_table.shape
    da = n_heads * hd
    assert MOBA_BLOCK % page == 0 and n_pages % (MOBA_BLOCK // page) == 0
    assert n_heads % 8 == 0 and LANES % n_heads == 0 and n_heads & (n_heads - 1) == 0 and hd == LANES
    G = next(g for g in (8, 4, 2, 1) if n_pages % g == 0)
    NG = n_pages // G
    w = page * n_heads
    diag = (jnp.arange(w)[None, :] % n_heads == jnp.arange(n_heads)[:, None]).astype(F32)

    def k_map(g):
        return lambda i, t, pt: (pt[i, jnp.minimum(t, NG - 1) * G + g], 0, 0)

    def v_map(g):
        return lambda i, t, pt: (pt[i, jnp.maximum(t - NG, 0) * G + g], 0, 0)

    ck = cache_k.reshape(n_pool, w, hd)
    cv = cache_v.reshape(n_pool, w, hd)
    per_b = lambda i, t, pt: (i, 0, 0)
    grid_spec = pltpu.PrefetchScalarGridSpec(
        num_scalar_prefetch=1,
        grid=(b, 2 * NG),
        in_specs=[pl.BlockSpec((1, w, hd), k_map(g)) for g in range(G)]
        + [pl.BlockSpec((1, w, hd), v_map(g)) for g in range(G)]
        + [pl.BlockSpec((1, n_heads, hd), per_b)] * 3
        + [pl.BlockSpec((n_heads, w), lambda i, t, pt: (0, 0))],
        out_specs=pl.BlockSpec((1, n_heads, hd), per_b),
        scratch_shapes=[pltpu.VMEM((NG, G, w), F32),
                        pltpu.VMEM((n_pages, n_heads, hd), F32),
                        pltpu.VMEM((NG, G, w), F32),
                        pltpu.VMEM((n_heads, hd), F32),
                        pltpu.VMEM((1, LANES), F32)])
    out = pl.pallas_call(
        functools.partial(_moba_sample_body, G, NG, page, hd ** -0.5),
        out_shape=jax.ShapeDtypeStruct((b, n_heads, hd), F32),
        grid_spec=grid_spec,
        compiler_params=_params("parallel", "arbitrary"),
        name="moba_sample",
    )(page_table, *([ck] * G), *([cv] * G), q.reshape(b, n_heads, hd), kn.reshape(b, n_heads, hd),
      vn.reshape(b, n_heads, hd), diag)
    return out.reshape(b, da)


def _rope_tables(pos, d_head):
    inv = ROPE_THETA ** (-jnp.arange(0, d_head, 2, dtype=F32) / d_head)
    ang = pos.astype(F32)[:, None] * inv[None, :]
    cos, sin = jnp.cos(ang), jnp.sin(ang)
    if d_head // 2 == LANES:
        return cos, sin
    assert d_head == LANES
    return jnp.concatenate([cos, cos], axis=1), jnp.concatenate([-sin, sin], axis=1)


def _ffn(h, g, w_in, w_out):
    a = _norm_matmul(h, g, w_in.astype(BF16), BF16, mode="swiglu")
    return _matmul_residual(a, w_out.astype(BF16), h, 0.5)


def kernel(x_prompt, x_sample, state_ret, cache_k, cache_v, page_table, norm_ffa, ffa_w_in, ffa_w_out, norm_mix, norm_ffb, ffb_w_in, ffb_w_out, ret_w_in, ret_w_out, kv_norm, w_kv, moba_w_q, moba_w_o, final_norm):
    bp, s_len, d = x_prompt.shape
    db, dl, _ = x_sample.shape
    n_a, _, ret_heads, dk, dv = state_ret.shape
    _, page, att_heads, hd = cache_k.shape
    assert bp == 1 and dl == 1 and n_a == 1 and norm_ffa.shape[0] == 2 and s_len % db == 0
    assert s_len % RET_CHUNK == 0 and s_len % MOBA_BLOCK == 0
    past_len = page_table.shape[1] * page
    qk_w, v_w, att_w = ret_heads * dk, ret_heads * dv, att_heads * hd

    pos = jnp.concatenate([jnp.arange(s_len, dtype=jnp.int32), jnp.full((db,), past_len, jnp.int32)])
    cos_r, sin_r = _rope_tables(pos, dk)
    cos_m, sin_m = _rope_tables(pos, hd)
    log_g = jnp.log(1.0 - 2.0 ** (-5.0 - jnp.arange(ret_heads, dtype=F32)))

    h = jnp.concatenate([x_prompt[0], x_sample[:, 0]], axis=0)

    h = _ffn(h, norm_ffa[0], ffa_w_in[0], ffa_w_out[0])
    w_ret = ret_w_in[0]
    q = _norm_matmul(h, norm_mix[0], w_ret[:, :qk_w].astype(BF16), F32, "rope", cos_r, sin_r, dk)
    k = _norm_matmul(h, norm_mix[0], w_ret[:, qk_w:2 * qk_w].astype(BF16), F32, "rope", cos_r, sin_r, dk, dk ** -0.5)
    vg = _norm_matmul(h, norm_mix[0], w_ret[:, 2 * qk_w:].astype(BF16), BF16)
    o_p, st_p = _retention_prompt(q, k, vg, s_len, ret_heads, dk, dv, log_g)
    vg_s = vg[s_len:].astype(F32)
    o_s, st_s = _retention_sample(q[s_len:], k[s_len:], vg_s[:, :v_w], vg_s[:, v_w:], state_ret[0], log_g)
    o = jnp.concatenate([o_p, o_s.reshape(db, v_w).astype(BF16)], axis=0)
    h = _matmul_residual(o, ret_w_out[0].astype(BF16), h, 1.0)
    h = _ffn(h, norm_ffb[0], ffb_w_in[0], ffb_w_out[0])

    w_k, w_v = w_kv[:, :att_w].astype(BF16), w_kv[:, att_w:].astype(BF16)
    prompt_rows, sample_rows = (0, s_len), (s_len, db)
    k_p = _norm_matmul(h, kv_norm, w_k, F32, "rope", cos_m, sin_m, hd, rows=prompt_rows)
    k_s = _norm_matmul(h, kv_norm, w_k, F32, "rope", cos_m, sin_m, hd, rows=sample_rows)
    v_p = _norm_matmul(h, kv_norm, w_v, F32, rows=prompt_rows)
    v_s = _norm_matmul(h, kv_norm, w_v, F32, rows=sample_rows)

    h = _ffn(h, norm_ffa[1], ffa_w_in[1], ffa_w_out[1])
    q = _norm_matmul(h, norm_mix[1], moba_w_q[0].astype(BF16), F32, "rope", cos_m, sin_m, hd)
    a_p = _moba_prompt(q, k_p, v_p, s_len, att_heads, hd)
    a_s = _moba_sample(q[s_len:], k_s, v_s, cache_k, cache_v, page_table)
    a = jnp.concatenate([a_p, a_s.astype(BF16)], axis=0)
    h = _matmul_residual(a, moba_w_o[0].astype(BF16), h, 1.0)
    h = _ffn(h, norm_ffb[1], ffb_w_in[1], ffb_w_out[1])
    y_p = _rmsnorm(h, final_norm, prompt_rows)
    y_s = _rmsnorm(h, final_norm, sample_rows)

    return (y_p.reshape(1, s_len, d),
            y_s.reshape(db, 1, d),
            st_p[None, None],
            st_s[None],
            k_p.reshape(1, s_len, att_heads, hd),
            v_p.reshape(1, s_len, att_heads, hd),
            k_s.reshape(db, 1, att_heads, hd),
            v_s.reshape(db, 1, att_heads, hd))
```

```python
import functools

import jax
import jax.numpy as jnp
from jax import lax
from jax.experimental import pallas as pl
from jax.experimental.pallas import tpu as pltpu

F32 = jnp.float32
BF16 = jnp.bfloat16

NORM_EPS = 1e-6
GN_EPS = 1e-5
NEG_INF = -1e30
ROPE_THETA = 10000.0
MOBA_BLOCK = 256
MOBA_TOPK = 3
RET_CHUNK = 128
LOG2_E = 1.4426950408889634
LANES = 128
VMEM_LIMIT = 56 * 1024 * 1024


def _params(*sem):
    return pltpu.CompilerParams(dimension_semantics=sem, vmem_limit_bytes=VMEM_LIMIT)


def _dot(a, b):
    return jnp.dot(a, b, preferred_element_type=F32)


def _dot_nt(a, b):
    return lax.dot_general(a, b, (((1,), (1,)), ((), ())), preferred_element_type=F32)


def _dot_tn(a, b):
    return lax.dot_general(a, b, (((0,), (0,)), ((), ())), preferred_element_type=F32)


def _split_hi_lo(x):
    hi = x.astype(BF16)
    lo = (x - hi.astype(F32)).astype(BF16)
    return hi, lo


def _silu(x):
    return x * jax.nn.sigmoid(x)


def _row_tile(m):
    for t in (1040, 1024, 640, 512, 256, 128, 64, 32, 16):
        if m % t == 0:
            return t
    raise ValueError(f"row count {m} has no supported tile")


def _col_tile(n, cap=1024):
    for t in (1024, 512, 256, 128):
        if t > cap:
            continue
        if n % t == 0:
            return t
    raise ValueError(f"column count {n} has no supported tile")


def _rms(x, g):
    return x * lax.rsqrt(jnp.mean(x * x, axis=-1, keepdims=True) + NORM_EPS) * g


def _store_cols(o_ref, hm_ref, km_ref, r, lo):
    width = r.shape[1]
    o_ref[:, lo:lo + width] = r.astype(o_ref.dtype)
    if hm_ref is not None:
        for hh in range(width // LANES):
            hm_ref[lo // LANES + hh] = r[:, hh * LANES:(hh + 1) * LANES].astype(BF16)
    if km_ref is not None:
        for b in range(r.shape[0] // MOBA_BLOCK):
            km_ref[b, :, lo:lo + width] = jnp.mean(r[b * MOBA_BLOCK:(b + 1) * MOBA_BLOCK, :], axis=0, keepdims=True)


def _rope_store(o_ref, hm_ref, km_ref, acc, cos, sin, d_head, scale):
    tn = acc.shape[1]
    half = d_head // 2
    for h in range(tn // d_head):
        lo = h * d_head
        if half % LANES == 0:
            x1 = acc[:, lo:lo + half]
            x2 = acc[:, lo + half:lo + d_head]
            r1 = x1 * cos - x2 * sin
            r2 = x2 * cos + x1 * sin
            if scale != 1.0:
                r1, r2 = r1 * scale, r2 * scale
            o_ref[:, lo:lo + half] = r1.astype(o_ref.dtype)
            o_ref[:, lo + half:lo + d_head] = r2.astype(o_ref.dtype)
        else:
            xh = acc[:, lo:lo + d_head]
            r = xh * cos + pltpu.roll(xh, half, 1) * sin
            if scale != 1.0:
                r = r * scale
            _store_cols(o_ref, hm_ref, km_ref, r, lo)


def _norm_mm_body(mode, d_head, scale, aux, x_ref, g_ref, *refs):
    xn_ref = refs[-1]

    @pl.when(pl.program_id(1) == 0)
    def _():
        xn_ref[...] = _rms(x_ref[...], g_ref[...]).astype(BF16)

    xn = xn_ref[...]
    if mode == "swiglu":
        wg_ref, wu_ref, o_ref = refs[:3]
        gate = _dot(xn, wg_ref[...])
        up = _dot(xn, wu_ref[...])
        o_ref[...] = (_silu(gate) * up).astype(o_ref.dtype)
    elif mode == "rope":
        w_ref, cos_ref, sin_ref, o_ref = refs[:4]
        hm_ref, km_ref = refs[4:6] if aux else (None, None)
        _rope_store(o_ref, hm_ref, km_ref, _dot(xn, w_ref[...]), cos_ref[...], sin_ref[...], d_head, scale)
    else:
        w_ref, o_ref = refs[:2]
        _store_cols(o_ref, refs[2] if aux else None, None, _dot(xn, w_ref[...]), 0)


def _norm_matmul(x, g, w, out_dtype, mode="plain", cos=None, sin=None, d_head=0, scale=1.0, rows=None, layer=0,
                 cols=None, aux=False):
    d = x.shape[1]
    first, m = rows if rows is not None else (0, x.shape[0])
    c_first, n = cols if cols is not None else (0, w.shape[2] // 2 if mode == "swiglu" else w.shape[2])
    tm, tn = _row_tile(m), _col_tile(n, cap=512 if mode == "swiglu" else 1024)
    assert first % tm == 0 and c_first % tn == 0
    i0, j0 = first // tm, c_first // tn
    nj = n // tn
    in_specs = [pl.BlockSpec((tm, d), lambda i, j: (i + i0, 0)), pl.BlockSpec((1, d), lambda i, j: (0, 0))]
    args = [x, g.reshape(1, d)]
    w_spec = lambda off: pl.BlockSpec((None, d, tn), lambda i, j: (layer, 0, j + off))
    if mode == "swiglu":
        in_specs += [w_spec(0), w_spec(nj)]
        args += [w, w]
    else:
        in_specs.append(w_spec(j0))
        args.append(w)
    if mode == "rope":
        assert tn % d_head == 0
        in_specs += [pl.BlockSpec((tm, LANES), lambda i, j: (i + i0, 0))] * 2
        args += [cos, sin]
    out_shape = jax.ShapeDtypeStruct((m, n), out_dtype)
    out_specs = pl.BlockSpec((tm, tn), lambda i, j: (i, j))
    if aux:
        assert mode in ("rope", "plain") and (mode == "plain" or d_head == LANES) and tm % MOBA_BLOCK == 0
        out_shape = [out_shape, jax.ShapeDtypeStruct((n // LANES, m, LANES), BF16)]
        out_specs = [out_specs, pl.BlockSpec((tn // LANES, tm, LANES), lambda i, j: (j, i, 0))]
        if mode == "rope":
            out_shape.append(jax.ShapeDtypeStruct((m // MOBA_BLOCK, 1, n), F32))
            out_specs.append(pl.BlockSpec((tm // MOBA_BLOCK, 1, tn), lambda i, j: (i, 0, j)))
    return pl.pallas_call(
        functools.partial(_norm_mm_body, mode, d_head, scale, aux),
        out_shape=out_shape,
        grid=(m // tm, nj),
        in_specs=in_specs,
        out_specs=out_specs,
        scratch_shapes=[pltpu.VMEM((tm, d), BF16)],
        compiler_params=_params("parallel", "arbitrary"),
        name=f"norm_matmul_{mode}",
    )(*args)


def _mm_res_body(scale, a_ref, w_ref, h_ref, o_ref):
    acc = _dot(a_ref[...], w_ref[...])
    o_ref[...] = h_ref[...] + (acc if scale == 1.0 else scale * acc)


def _matmul_residual(a, w, h, scale, layer=0):
    m, k = a.shape
    n = w.shape[2]
    tm, tn = _row_tile(m), _col_tile(n, cap=512)
    return pl.pallas_call(
        functools.partial(_mm_res_body, scale),
        out_shape=jax.ShapeDtypeStruct((m, n), F32),
        grid=(m // tm, n // tn),
        in_specs=[pl.BlockSpec((tm, k), lambda i, j: (i, 0)),
                  pl.BlockSpec((None, k, tn), lambda i, j: (layer, 0, j)),
                  pl.BlockSpec((tm, tn), lambda i, j: (i, j))],
        out_specs=pl.BlockSpec((tm, tn), lambda i, j: (i, j)),
        compiler_params=_params("parallel", "arbitrary"),
        name="matmul_residual",
    )(a, w, h)


def _rmsnorm_body(x_ref, g_ref, o_ref):
    o_ref[...] = _rms(x_ref[...], g_ref[...])


def _rmsnorm(x, g, rows):
    d = x.shape[1]
    first, m = rows
    tm = _row_tile(m)
    assert first % tm == 0
    i0 = first // tm
    return pl.pallas_call(
        _rmsnorm_body,
        out_shape=jax.ShapeDtypeStruct((m, d), F32),
        grid=(m // tm,),
        in_specs=[pl.BlockSpec((tm, d), lambda i: (i + i0, 0)), pl.BlockSpec((1, d), lambda i: (0, 0))],
        out_specs=pl.BlockSpec((tm, d), lambda i: (i, 0)),
        compiler_params=_params("parallel"),
        name="final_rmsnorm",
    )(x, g.reshape(1, d))


def _group_norm_gate(o, g):
    mu = jnp.mean(o, axis=-1, keepdims=True)
    d = o - mu
    var = jnp.mean(d * d, axis=-1, keepdims=True)
    return d * lax.rsqrt(var + GN_EPS) * _silu(g)


def _ret_prompt_body(q_ref, k_ref, v_ref, g_ref, dmat_ref, cdec_ref, kdec_ref, gl_ref, o_ref, sfin_ref, s_scr):
    c = pl.program_id(1)

    @pl.when(c == 0)
    def _():
        s_scr[...] = jnp.zeros_like(s_scr)

    L = dmat_ref.shape[1]
    s = s_scr[...]
    for u in range(q_ref.shape[0] // L):
        rows = slice(u * L, (u + 1) * L)
        q = q_ref[rows, :].astype(BF16)
        k = k_ref[rows, :]
        v = v_ref[rows, :]
        a = _dot_nt(q, k.astype(BF16)) * dmat_ref[0]
        inner = _dot(a.astype(BF16), v)
        cross = _dot(q, s.astype(BF16)) * cdec_ref[0]
        kd = (k * kdec_ref[0]).astype(BF16)
        s = gl_ref[0] * s + _dot_tn(kd, v)
        o_ref[rows, :] = _group_norm_gate(inner + cross, g_ref[rows, :].astype(F32)).astype(o_ref.dtype)
    s_scr[...] = s

    @pl.when(c == pl.num_programs(1) - 1)
    def _():
        sfin_ref[0] = s


def _retention_prompt(q, k, vg, s_len, n_heads, dk, dv, log_g):
    L = RET_CHUNK
    per_step = 4 if s_len % (4 * L) == 0 else 1
    T = per_step * L
    i = jnp.arange(L, dtype=F32)
    diff = i[:, None] - i[None, :]
    dmat = jnp.where(diff >= 0, jnp.exp(log_g[:, None, None] * jnp.maximum(diff, 0.0)), 0.0)
    cdec = jnp.exp(log_g[:, None] * (i[None, :] + 1.0))[:, :, None]
    kdec = jnp.exp(log_g[:, None] * (L - 1.0 - i)[None, :])[:, :, None]
    gl = jnp.broadcast_to(jnp.exp(log_g * L)[:, None, None], (n_heads, 1, dv))
    return pl.pallas_call(
        _ret_prompt_body,
        out_shape=(jax.ShapeDtypeStruct((s_len, n_heads * dv), BF16),
                   jax.ShapeDtypeStruct((n_heads, dk, dv), F32)),
        grid=(n_heads, s_len // T),
        in_specs=[pl.BlockSpec((T, dk), lambda h, c: (c, h)),
                  pl.BlockSpec((T, dk), lambda h, c: (c, h)),
                  pl.BlockSpec((T, dv), lambda h, c: (c, h)),
                  pl.BlockSpec((T, dv), lambda h, c: (c, h + n_heads)),
                  pl.BlockSpec((1, L, L), lambda h, c: (h, 0, 0)),
                  pl.BlockSpec((1, L, 1), lambda h, c: (h, 0, 0)),
                  pl.BlockSpec((1, L, 1), lambda h, c: (h, 0, 0)),
                  pl.BlockSpec((1, 1, dv), lambda h, c: (h, 0, 0))],
        out_specs=(pl.BlockSpec((T, dv), lambda h, c: (c, h)),
                   pl.BlockSpec((1, dk, dv), lambda h, c: (h, 0, 0))),
        scratch_shapes=[pltpu.VMEM((dk, dv), F32)],
        compiler_params=_params("parallel", "arbitrary"),
        name="retention_prompt",
    )(q, k, vg, vg, dmat, cdec, kdec, gl)


def _ret_sample_body(n_heads, qc_ref, kc_ref, v_ref, g_ref, g1_ref, s_ref, o_ref, snew_ref):
    qc = qc_ref[0]
    kc = kc_ref[0]
    for h in range(n_heads):
        s = s_ref[0, h]
        qh = qc[:, h:h + 1]
        kh = kc[:, h:h + 1]
        vh = v_ref[0, h:h + 1, :]
        g1 = g1_ref[h:h + 1, :]
        a = jnp.sum(qh * kh, axis=0, keepdims=True)
        cross = jnp.sum(s * qh, axis=0, keepdims=True) * g1
        snew_ref[0, h] = g1 * s + kh * vh
        o = a * vh + cross
        o_ref[0, h:h + 1, :] = _group_norm_gate(o, g_ref[0, h:h + 1, :])


def _retention_sample(q, k, v, g, state, log_g):
    b, n_heads, dk, dv = state.shape
    qc = q.reshape(b, n_heads, dk).transpose(0, 2, 1)
    kc = k.reshape(b, n_heads, dk).transpose(0, 2, 1)
    g1 = jnp.broadcast_to(jnp.exp(log_g * 1.0)[:, None], (n_heads, dv))
    return pl.pallas_call(
        functools.partial(_ret_sample_body, n_heads),
        out_shape=(jax.ShapeDtypeStruct((b, n_heads, dv), F32),
                   jax.ShapeDtypeStruct((b, n_heads, dk, dv), F32)),
        grid=(b,),
        in_specs=[pl.BlockSpec((1, dk, n_heads), lambda i: (i, 0, 0)),
                  pl.BlockSpec((1, dk, n_heads), lambda i: (i, 0, 0)),
                  pl.BlockSpec((1, n_heads, dv), lambda i: (i, 0, 0)),
                  pl.BlockSpec((1, n_heads, dv), lambda i: (i, 0, 0)),
                  pl.BlockSpec((n_heads, dv), lambda i: (0, 0)),
                  pl.BlockSpec((1, n_heads, dk, dv), lambda i: (i, 0, 0, 0))],
        out_specs=(pl.BlockSpec((1, n_heads, dv), lambda i: (i, 0, 0)),
                   pl.BlockSpec((1, n_heads, dk, dv), lambda i: (i, 0, 0, 0))),
        compiler_params=_params("parallel"),
        name="retention_sample",
    )(qc, kc, v.reshape(b, n_heads, dv), g.reshape(b, n_heads, dv), g1, state)


def _select_topk(s, idx, n_valid, axis):
    big = float(s.shape[axis])
    s = jnp.where(idx < n_valid, s, NEG_INF)
    sel = jnp.zeros(s.shape, F32)
    for j in range(min(MOBA_TOPK, s.shape[axis])):
        m = jnp.max(s, axis=axis, keepdims=True)
        first = jnp.min(jnp.where(s == m, idx, big), axis=axis, keepdims=True)
        pick = idx == first
        sel = jnp.where(pick, jnp.where(n_valid > j, 1.0, 0.0), sel)
        s = jnp.where(pick, -jnp.inf, s)
    return sel


def _moba_prompt_body(nb, scale, q_ref, k_ref, v_ref, kmean_ref, o_ref, kmean_scr, vt_scr, sel_scr, m_scr, l_scr,
                      acc_scr, s_buf, p_buf, a_buf):
    blk = MOBA_BLOCK
    sub = blk // 8
    hd = acc_scr.shape[0]
    qb = pl.program_id(1)

    @pl.when(qb == 0)
    def _():
        for n in range(nb):
            kmean_scr[n:n + 1, :] = kmean_ref[n]
            vt_scr[n] = v_ref[n * blk:(n + 1) * blk, :].astype(F32).T.astype(BF16)

    qt = q_ref[...].T
    qt_hi, qt_lo = _split_hi_lo(qt)
    qt_scaled = (qt * (scale * LOG2_E)).astype(BF16)
    km_hi, km_lo = _split_hi_lo(kmean_scr[...])
    s_blk = _dot(km_hi, qt_lo) + _dot(km_lo, qt_hi) + _dot(km_hi, qt_hi)
    bidx = lax.broadcasted_iota(jnp.int32, (nb, blk), 0).astype(F32)
    sel = _select_topk(s_blk, bidx, qb.astype(F32), axis=0)
    for n in range(nb):
        sel_scr[n] = jnp.broadcast_to(sel[n:n + 1, :], (8, blk))

    m_scr[...] = jnp.full(m_scr.shape, NEG_INF, F32)
    l_scr[...] = jnp.zeros_like(l_scr)
    acc_scr[...] = jnp.zeros_like(acc_scr)

    def scores(n):
        kb = k_ref[pl.ds(pl.multiple_of(n * blk, blk), blk), :]
        return _dot(kb, qt_scaled).reshape(sub, 8, blk)

    def softmax_update(ss):
        m_old = m_scr[...]
        mx = functools.reduce(jnp.maximum, [jnp.max(s, axis=0) for s in ss])
        m_new = jnp.maximum(m_old, jnp.max(mx, axis=0, keepdims=True))
        alpha = jnp.exp2(m_old - m_new)
        ps = [jnp.exp2(s - m_new[None]) for s in ss]
        psum = functools.reduce(jnp.add, [jnp.sum(p, axis=0) for p in ps])
        l_scr[...] = alpha * l_scr[...] + jnp.sum(psum, axis=0, keepdims=True)
        m_scr[...] = m_new
        return alpha, [p.reshape(blk, blk).astype(BF16) for p in ps]

    def accumulate(alpha, blocks, pbs):
        pv = functools.reduce(jnp.add, [_dot(vt_scr[n], pb) for n, pb in zip(blocks, pbs)])
        acc_scr[...] = (acc_scr[...].reshape(hd // 8, 8, blk) * alpha[None]).reshape(hd, blk) + pv

    kpos = (lax.broadcasted_iota(jnp.int32, (sub, 8, blk), 0) * 8
            + lax.broadcasted_iota(jnp.int32, (sub, 8, blk), 1))
    qpos = lax.broadcasted_iota(jnp.int32, (sub, 8, blk), 2)
    alpha, pbs = softmax_update([jnp.where(kpos <= qpos, scores(qb), NEG_INF)])
    accumulate(alpha, [qb], pbs)

    n_items = (qb + 1) // 2
    s_buf[1] = jnp.full(s_buf.shape[1:], NEG_INF, F32)
    p_buf[0] = jnp.zeros(p_buf.shape[1:], BF16)
    a_buf[0] = jnp.ones(a_buf.shape[1:], F32)

    def step(t, slot):
        prev = 1 - slot
        b1 = jnp.minimum(2 * t, nb - 2)
        for j in range(2):
            row = jnp.where(t < n_items, sel_scr[b1 + j], 0.0)
            s_buf[slot, j] = jnp.where(row[None] > 0.5, scores(b1 + j), NEG_INF)
        alpha, pbs = softmax_update([s_buf[prev, 0], s_buf[prev, 1]])
        a_buf[prev] = alpha
        p_buf[prev, 0] = pbs[0]
        p_buf[prev, 1] = pbs[1]
        b3 = jnp.clip(2 * (t - 2), 0, nb - 2)
        accumulate(a_buf[slot], [b3, b3 + 1], [p_buf[slot, 0], p_buf[slot, 1]])

    def two_steps(u, carry):
        step(2 * u, 0)
        step(2 * u + 1, 1)
        return carry

    lax.fori_loop(0, jnp.where(n_items > 0, (n_items + 3) // 2, 0), two_steps, 0)

    inv_l = 1.0 / l_scr[...]
    o = (acc_scr[...].reshape(hd // 8, 8, blk) * inv_l[None]).reshape(hd, blk)
    o_ref[...] = o.T.astype(o_ref.dtype)


def _moba_prompt(q, k, v, kmean, s_len, n_heads, hd):
    assert s_len % MOBA_BLOCK == 0 and s_len >= 2 * MOBA_BLOCK
    nb = s_len // MOBA_BLOCK
    blk = MOBA_BLOCK
    return pl.pallas_call(
        functools.partial(_moba_prompt_body, nb, hd ** -0.5),
        out_shape=jax.ShapeDtypeStruct((s_len, n_heads * hd), BF16),
        grid=(n_heads, nb),
        in_specs=[pl.BlockSpec((blk, hd), lambda h, i: (i, h)),
                  pl.BlockSpec((None, s_len, hd), lambda h, i: (h, 0, 0)),
                  pl.BlockSpec((None, s_len, hd), lambda h, i: (h, 0, 0)),
                  pl.BlockSpec((nb, 1, hd), lambda h, i: (0, 0, h))],
        out_specs=pl.BlockSpec((blk, hd), lambda h, i: (i, h)),
        scratch_shapes=[pltpu.VMEM((nb, hd), F32),
                        pltpu.VMEM((nb, hd, blk), BF16),
                        pltpu.VMEM((nb, 8, blk), F32),
                        pltpu.VMEM((8, blk), F32),
                        pltpu.VMEM((8, blk), F32),
                        pltpu.VMEM((hd, blk), F32),
                        pltpu.VMEM((2, 2, blk // 8, 8, blk), F32),
                        pltpu.VMEM((2, 2, blk, blk), BF16),
                        pltpu.VMEM((2, 8, blk), F32)],
        compiler_params=_params("parallel", "arbitrary"),
        name="moba_prompt",
    )(q, k, v, kmean)


def _class_reduce(x, op, period):
    w = x.shape[1]
    y = functools.reduce(op, [x[:, i * LANES:(i + 1) * LANES] for i in range(w // LANES)])
    y = jnp.broadcast_to(y, (8, LANES))
    shift = LANES // 2
    while shift >= period:
        y = op(y, pltpu.roll(y, shift, 1))
        shift //= 2
    return jnp.concatenate([y[0:1]] * (w // LANES), axis=1)


def _block_rows(s, n_blocks, period):
    per_tile = LANES // period
    rows = []
    for n in range(n_blocks):
        tile = jnp.broadcast_to(s[:, (n // per_tile) * LANES:(n // per_tile + 1) * LANES], (8, LANES))
        shift = (LANES - (n % per_tile) * period) % LANES
        rows.append((pltpu.roll(tile, shift, 1) if shift else tile)[0:1])
    return jnp.concatenate(rows, axis=0)


def _replicate_classes(x, period):
    lane = lax.broadcasted_iota(jnp.int32, x.shape, 1)
    x = jnp.where(lane < period, x, 0.0)
    shift = period
    while shift < LANES:
        x = x + pltpu.roll(x, shift, 1)
        shift *= 2
    return x


def _moba_sample_body(G, NG, page, scale, pt_ref, *refs):
    ck = refs[:G]
    cv = refs[G:2 * G]
    q_ref, kn_ref, vn_ref, diag_ref, o_ref, s_all, psum, p_all, acc, pnew = refs[2 * G:]
    H, hd = q_ref.shape[1:]
    W = page * H
    NP = G * NG
    ppb = MOBA_BLOCK // page
    cb = NP // ppb
    t = pl.program_id(1)

    def diag_rows(res):
        return jnp.sum(res * diag_ref[:, :res.shape[1]], axis=0, keepdims=True)

    def pad_rows(x, rows):
        return jnp.concatenate([x, jnp.zeros((rows - x.shape[0], x.shape[1]), x.dtype)], axis=0)

    @pl.when(t < NG)
    def _k_phase():
        q_bf = q_ref[0].astype(BF16)
        for g in range(G):
            kp = ck[g][0]
            psum[t * G + g] = jnp.sum(kp.reshape(page, H, hd), axis=0)
            s_all[t, g:g + 1, :] = diag_rows(_dot_nt(q_bf, kp.astype(BF16))) * scale

    @pl.when(t == NG - 1)
    def _select_and_softmax():
        q_hi, q_lo = _split_hi_lo(q_ref[0])
        kn = kn_ref[0]
        kmean = [sum(psum[n * ppb + j] for j in range(ppb)) / MOBA_BLOCK for n in range(cb)] + [kn / MOBA_BLOCK]
        wb = -(-(cb + 1) * H // LANES) * LANES
        km_hi, km_lo = _split_hi_lo(pad_rows(jnp.concatenate(kmean, axis=0), wb))
        s_blk = diag_rows(_dot_nt(q_lo, km_hi) + _dot_nt(q_hi, km_lo) + _dot_nt(q_hi, km_hi))
        s_rows = pad_rows(_block_rows(s_blk, cb + 1, H), -(-(cb + 1) // 8) * 8)
        bidx = lax.broadcasted_iota(jnp.int32, s_rows.shape, 0).astype(F32)
        sel = _replicate_classes(_select_topk(s_rows, bidx, float(cb), axis=0), H)
        s_new = _class_reduce(diag_rows(_dot_nt(q_hi, pad_rows(kn, LANES).astype(BF16))), jnp.add, H) * scale
        s_new = jnp.concatenate([s_new] * (W // LANES), axis=1)
        m = None
        for pg in range(NP):
            chosen = jnp.concatenate([sel[pg // ppb:pg // ppb + 1, :]] * (W // LANES), axis=1)
            sm = jnp.where(chosen > 0.5, s_all[pg // G, pg % G:pg % G + 1, :], NEG_INF)
            s_all[pg // G, pg % G:pg % G + 1, :] = sm
            m = sm if m is None else jnp.maximum(m, sm)
        m = jnp.maximum(_class_reduce(m, jnp.maximum, H), s_new)
        l = None
        for pg in range(NP):
            p = jnp.exp(s_all[pg // G, pg % G:pg % G + 1, :] - m)
            s_all[pg // G, pg % G:pg % G + 1, :] = p
            l = p if l is None else l + p
        p_new = jnp.exp(s_new - m)
        inv = 1.0 / (_class_reduce(l, jnp.add, H) + p_new)
        for pg in range(NP):
            p_all[pg // G, pg % G:pg % G + 1, :] = s_all[pg // G, pg % G:pg % G + 1, :] * inv
        pnew[...] = (p_new * inv)[:, :LANES]

    @pl.when(t == NG)
    def _():
        acc[...] = jnp.zeros_like(acc)

    @pl.when(t >= NG)
    def _v_phase():
        for g in range(G):
            pm = (jnp.broadcast_to(p_all[t - NG, g:g + 1, :], (H, W)) * diag_ref[...]).astype(BF16)
            acc[...] += _dot(pm, cv[g][0].astype(BF16))

    @pl.when(t == 2 * NG - 1)
    def _():
        pm = (jnp.broadcast_to(pnew[...], (H, LANES)) * diag_ref[:, :LANES]).astype(BF16)
        o_ref[0] = acc[...] + _dot(pm, pad_rows(vn_ref[0], LANES).astype(BF16))


def _moba_sample(q, kn, vn, cache_k, cache_v, page_table):
    n_pool, page, n_heads, hd = cache_k.shape
    b, n_pages = page_table.shape
    da = n_heads * hd
    assert MOBA_BLOCK % page == 0 and n_pages % (MOBA_BLOCK // page) == 0
    assert n_heads % 8 == 0 and LANES % n_heads == 0 and n_heads & (n_heads - 1) == 0 and hd == LANES
    G = next(g for g in (8, 4, 2, 1) if n_pages % g == 0)
    NG = n_pages // G
    w = page * n_heads
    diag = (jnp.arange(w)[None, :] % n_heads == jnp.arange(n_heads)[:, None]).astype(F32)

    def k_map(g):
        return lambda i, t, pt: (pt[i, jnp.minimum(t, NG - 1) * G + g], 0, 0)

    def v_map(g):
        return lambda i, t, pt: (pt[i, jnp.maximum(t - NG, 0) * G + g], 0, 0)

    ck = cache_k.reshape(n_pool, w, hd)
    cv = cache_v.reshape(n_pool, w, hd)
    per_b = lambda i, t, pt: (i, 0, 0)
    grid_spec = pltpu.PrefetchScalarGridSpec(
        num_scalar_prefetch=1,
        grid=(b, 2 * NG),
        in_specs=[pl.BlockSpec((1, w, hd), k_map(g)) for g in range(G)]
        + [pl.BlockSpec((1, w, hd), v_map(g)) for g in range(G)]
        + [pl.BlockSpec((1, n_heads, hd), per_b)] * 3
        + [pl.BlockSpec((n_heads, w), lambda i, t, pt: (0, 0))],
        out_specs=pl.BlockSpec((1, n_heads, hd), per_b),
        scratch_shapes=[pltpu.VMEM((NG, G, w), F32),
                        pltpu.VMEM((n_pages, n_heads, hd), F32),
                        pltpu.VMEM((NG, G, w), F32),
                        pltpu.VMEM((n_heads, hd), F32),
                        pltpu.VMEM((1, LANES), F32)])
    out = pl.pallas_call(
        functools.partial(_moba_sample_body, G, NG, page, hd ** -0.5),
        out_shape=jax.ShapeDtypeStruct((b, n_heads, hd), F32),
        grid_spec=grid_spec,
        compiler_params=_params("parallel", "arbitrary"),
        name="moba_sample",
    )(page_table, *([ck] * G), *([cv] * G), q.reshape(b, n_heads, hd), kn.reshape(b, n_heads, hd),
      vn.reshape(b, n_heads, hd), diag)
    return out.reshape(b, da)


def _rope_tables(pos, d_head):
    inv = ROPE_THETA ** (-jnp.arange(0, d_head, 2, dtype=F32) / d_head)
    ang = pos.astype(F32)[:, None] * inv[None, :]
    cos, sin = jnp.cos(ang), jnp.sin(ang)
    if d_head // 2 == LANES:
        return cos, sin
    assert d_head == LANES
    return jnp.concatenate([cos, cos], axis=1), jnp.concatenate([-sin, sin], axis=1)


def _ffn(h, g, w_in, w_out, layer):
    a = _norm_matmul(h, g[layer], w_in, BF16, mode="swiglu", layer=layer)
    return _matmul_residual(a, w_out, h, 0.5, layer=layer)


def kernel(x_prompt, x_sample, state_ret, cache_k, cache_v, page_table, norm_ffa, ffa_w_in, ffa_w_out, norm_mix, norm_ffb, ffb_w_in, ffb_w_out, ret_w_in, ret_w_out, kv_norm, w_kv, moba_w_q, moba_w_o, final_norm):
    bp, s_len, d = x_prompt.shape
    db, dl, _ = x_sample.shape
    n_a, _, ret_heads, dk, dv = state_ret.shape
    _, page, att_heads, hd = cache_k.shape
    assert bp == 1 and dl == 1 and n_a == 1 and norm_ffa.shape[0] == 2 and s_len % db == 0
    assert s_len % RET_CHUNK == 0 and s_len % MOBA_BLOCK == 0
    past_len = page_table.shape[1] * page
    qk_w, v_w, att_w = ret_heads * dk, ret_heads * dv, att_heads * hd

    pos = jnp.concatenate([jnp.arange(s_len, dtype=jnp.int32), jnp.full((db,), past_len, jnp.int32)])
    cos_r, sin_r = _rope_tables(pos, dk)
    cos_m, sin_m = _rope_tables(pos, hd)
    log_g = jnp.log(1.0 - 2.0 ** (-5.0 - jnp.arange(ret_heads, dtype=F32)))

    ffa_w_in, ffa_w_out, ffb_w_in, ffb_w_out, ret_w_in, ret_w_out, moba_w_q, moba_w_o, w_kv3 = (
        p.astype(BF16) for p in (ffa_w_in, ffa_w_out, ffb_w_in, ffb_w_out, ret_w_in, ret_w_out, moba_w_q, moba_w_o,
                                 w_kv[None]))

    h = jnp.concatenate([x_prompt[0], x_sample[:, 0]], axis=0)

    h = _ffn(h, norm_ffa, ffa_w_in, ffa_w_out, 0)
    q = _norm_matmul(h, norm_mix[0], ret_w_in, F32, "rope", cos_r, sin_r, dk, cols=(0, qk_w))
    k = _norm_matmul(h, norm_mix[0], ret_w_in, F32, "rope", cos_r, sin_r, dk, dk ** -0.5, cols=(qk_w, qk_w))
    vg = _norm_matmul(h, norm_mix[0], ret_w_in, BF16, cols=(2 * qk_w, 2 * v_w))
    o_p, st_p = _retention_prompt(q, k, vg, s_len, ret_heads, dk, dv, log_g)
    vg_s = vg[s_len:].astype(F32)
    o_s, st_s = _retention_sample(q[s_len:], k[s_len:], vg_s[:, :v_w], vg_s[:, v_w:], state_ret[0], log_g)
    o = jnp.concatenate([o_p, o_s.reshape(db, v_w).astype(BF16)], axis=0)
    h = _matmul_residual(o, ret_w_out, h, 1.0)
    h = _ffn(h, norm_ffb, ffb_w_in, ffb_w_out, 0)

    k_cols, v_cols = (0, att_w), (att_w, att_w)
    prompt_rows, sample_rows = (0, s_len), (s_len, db)
    k_p, k_hm, k_mean = _norm_matmul(h, kv_norm, w_kv3, F32, "rope", cos_m, sin_m, hd, rows=prompt_rows, cols=k_cols,
                                     aux=True)
    k_s = _norm_matmul(h, kv_norm, w_kv3, F32, "rope", cos_m, sin_m, hd, rows=sample_rows, cols=k_cols)
    v_p, v_hm = _norm_matmul(h, kv_norm, w_kv3, F32, rows=prompt_rows, cols=v_cols, aux=True)
    v_s = _norm_matmul(h, kv_norm, w_kv3, F32, rows=sample_rows, cols=v_cols)

    h = _ffn(h, norm_ffa, ffa_w_in, ffa_w_out, 1)
    q = _norm_matmul(h, norm_mix[1], moba_w_q, F32, "rope", cos_m, sin_m, hd)
    a_p = _moba_prompt(q, k_hm, v_hm, k_mean, s_len, att_heads, hd)
    a_s = _moba_sample(q[s_len:], k_s, v_s, cache_k, cache_v, page_table)
    a = jnp.concatenate([a_p, a_s.astype(BF16)], axis=0)
    h = _matmul_residual(a, moba_w_o, h, 1.0)
    h = _ffn(h, norm_ffb, ffb_w_in, ffb_w_out, 1)
    y_p = _rmsnorm(h, final_norm, prompt_rows)
    y_s = _rmsnorm(h, final_norm, sample_rows)

    return (y_p.reshape(1, s_len, d),
            y_s.reshape(db, 1, d),
            st_p[None, None],
            st_s[None],
            k_p.reshape(1, s_len, att_heads, hd),
            v_p.reshape(1, s_len, att_heads, hd),
            k_s.reshape(db, 1, att_heads, hd),
            v_s.reshape(db, 1, att_heads, hd))
```

```python
import functools

import jax
import jax.numpy as jnp
from jax import lax
from jax.experimental import pallas as pl
from jax.experimental.pallas import tpu as pltpu

F32 = jnp.float32
BF16 = jnp.bfloat16

NORM_EPS = 1e-6
GN_EPS = 1e-5
NEG_INF = -1e30
ROPE_THETA = 10000.0
MOBA_BLOCK = 256
MOBA_TOPK = 3
RET_CHUNK = 128
LOG2_E = 1.4426950408889634
LANES = 128
VMEM_LIMIT = 56 * 1024 * 1024


def _params(*sem):
    return pltpu.CompilerParams(dimension_semantics=sem, vmem_limit_bytes=VMEM_LIMIT)


def _dot(a, b):
    return jnp.dot(a, b, preferred_element_type=F32)


def _dot_nt(a, b):
    return lax.dot_general(a, b, (((1,), (1,)), ((), ())), preferred_element_type=F32)


def _dot_tn(a, b):
    return lax.dot_general(a, b, (((0,), (0,)), ((), ())), preferred_element_type=F32)


def _split_hi_lo(x):
    hi = x.astype(BF16)
    lo = (x - hi.astype(F32)).astype(BF16)
    return hi, lo


def _silu(x):
    return x * jax.nn.sigmoid(x)


def _row_tile(m):
    for t in (1040, 1024, 640, 512, 256, 128, 64, 32, 16):
        if m % t == 0:
            return t
    raise ValueError(f"row count {m} has no supported tile")


def _col_tile(n, cap=1024):
    for t in (1024, 512, 256, 128):
        if t > cap:
            continue
        if n % t == 0:
            return t
    raise ValueError(f"column count {n} has no supported tile")


def _rms(x, g):
    return x * lax.rsqrt(jnp.mean(x * x, axis=-1, keepdims=True) + NORM_EPS) * g


def _store_cols(o_ref, hm_ref, km_ref, r, lo):
    width = r.shape[1]
    o_ref[:, lo:lo + width] = r.astype(o_ref.dtype)
    if hm_ref is not None:
        for hh in range(width // LANES):
            hm_ref[lo // LANES + hh] = r[:, hh * LANES:(hh + 1) * LANES].astype(BF16)
    if km_ref is not None:
        for b in range(r.shape[0] // MOBA_BLOCK):
            km_ref[b, :, lo:lo + width] = jnp.mean(r[b * MOBA_BLOCK:(b + 1) * MOBA_BLOCK, :], axis=0, keepdims=True)


def _rope_store(o_ref, hm_ref, km_ref, acc, cos, sin, d_head, scale):
    tn = acc.shape[1]
    half = d_head // 2
    for h in range(tn // d_head):
        lo = h * d_head
        if half % LANES == 0:
            x1 = acc[:, lo:lo + half]
            x2 = acc[:, lo + half:lo + d_head]
            r1 = x1 * cos - x2 * sin
            r2 = x2 * cos + x1 * sin
            if scale != 1.0:
                r1, r2 = r1 * scale, r2 * scale
            o_ref[:, lo:lo + half] = r1.astype(o_ref.dtype)
            o_ref[:, lo + half:lo + d_head] = r2.astype(o_ref.dtype)
        else:
            xh = acc[:, lo:lo + d_head]
            r = xh * cos + pltpu.roll(xh, half, 1) * sin
            if scale != 1.0:
                r = r * scale
            _store_cols(o_ref, hm_ref, km_ref, r, lo)


def _norm_mm_body(mode, d_head, scale, aux, x_ref, g_ref, *refs):
    xn_ref = refs[-1]

    @pl.when(pl.program_id(1) == 0)
    def _():
        xn_ref[...] = _rms(x_ref[...], g_ref[...]).astype(BF16)

    xn = xn_ref[...]
    if mode == "swiglu":
        wg_ref, wu_ref, o_ref = refs[:3]
        gate = _dot(xn, wg_ref[...].astype(BF16))
        up = _dot(xn, wu_ref[...].astype(BF16))
        o_ref[...] = (_silu(gate) * up).astype(o_ref.dtype)
    elif mode == "rope":
        w_ref, cos_ref, sin_ref, o_ref = refs[:4]
        hm_ref, km_ref = refs[4:6] if aux else (None, None)
        acc = _dot(xn, w_ref[...].astype(BF16))
        _rope_store(o_ref, hm_ref, km_ref, acc, cos_ref[...], sin_ref[...], d_head, scale)
    else:
        w_ref, o_ref = refs[:2]
        _store_cols(o_ref, refs[2] if aux else None, None, _dot(xn, w_ref[...].astype(BF16)), 0)


def _norm_matmul(x, g, w, out_dtype, mode="plain", cos=None, sin=None, d_head=0, scale=1.0, rows=None, layer=0,
                 cols=None, aux=False):
    d = x.shape[1]
    first, m = rows if rows is not None else (0, x.shape[0])
    c_first, n = cols if cols is not None else (0, w.shape[2] // 2 if mode == "swiglu" else w.shape[2])
    tm, tn = _row_tile(m), _col_tile(n, cap=512)
    assert first % tm == 0 and c_first % tn == 0
    i0, j0 = first // tm, c_first // tn
    nj = n // tn
    in_specs = [pl.BlockSpec((tm, d), lambda i, j: (i + i0, 0)), pl.BlockSpec((1, d), lambda i, j: (0, 0))]
    args = [x, g.reshape(1, d)]
    w_spec = lambda off: pl.BlockSpec((None, d, tn), lambda i, j: (layer, 0, j + off))
    if mode == "swiglu":
        in_specs += [w_spec(0), w_spec(nj)]
        args += [w, w]
    else:
        in_specs.append(w_spec(j0))
        args.append(w)
    if mode == "rope":
        assert tn % d_head == 0
        in_specs += [pl.BlockSpec((tm, LANES), lambda i, j: (i + i0, 0))] * 2
        args += [cos, sin]
    out_shape = jax.ShapeDtypeStruct((m, n), out_dtype)
    out_specs = pl.BlockSpec((tm, tn), lambda i, j: (i, j))
    if aux:
        assert mode in ("rope", "plain") and (mode == "plain" or d_head == LANES) and tm % MOBA_BLOCK == 0
        out_shape = [out_shape, jax.ShapeDtypeStruct((n // LANES, m, LANES), BF16)]
        out_specs = [out_specs, pl.BlockSpec((tn // LANES, tm, LANES), lambda i, j: (j, i, 0))]
        if mode == "rope":
            out_shape.append(jax.ShapeDtypeStruct((m // MOBA_BLOCK, 1, n), F32))
            out_specs.append(pl.BlockSpec((tm // MOBA_BLOCK, 1, tn), lambda i, j: (i, 0, j)))
    return pl.pallas_call(
        functools.partial(_norm_mm_body, mode, d_head, scale, aux),
        out_shape=out_shape,
        grid=(m // tm, nj),
        in_specs=in_specs,
        out_specs=out_specs,
        scratch_shapes=[pltpu.VMEM((tm, d), BF16)],
        compiler_params=_params("parallel", "arbitrary"),
        name=f"norm_matmul_{mode}",
    )(*args)


def _mm_res_body(scale, a_ref, w_ref, h_ref, o_ref):
    acc = _dot(a_ref[...], w_ref[...])
    o_ref[...] = h_ref[...] + (acc if scale == 1.0 else scale * acc)


def _matmul_residual(a, w, h, scale, layer=0):
    m, k = a.shape
    n = w.shape[2]
    tm, tn = _row_tile(m), _col_tile(n, cap=512)
    return pl.pallas_call(
        functools.partial(_mm_res_body, scale),
        out_shape=jax.ShapeDtypeStruct((m, n), F32),
        grid=(m // tm, n // tn),
        in_specs=[pl.BlockSpec((tm, k), lambda i, j: (i, 0)),
                  pl.BlockSpec((None, k, tn), lambda i, j: (layer, 0, j)),
                  pl.BlockSpec((tm, tn), lambda i, j: (i, j))],
        out_specs=pl.BlockSpec((tm, tn), lambda i, j: (i, j)),
        compiler_params=_params("parallel", "arbitrary"),
        name="matmul_residual",
    )(a, w, h)


def _rmsnorm_body(x_ref, g_ref, o_ref):
    o_ref[...] = _rms(x_ref[...], g_ref[...])


def _rmsnorm(x, g, rows):
    d = x.shape[1]
    first, m = rows
    tm = _row_tile(m)
    assert first % tm == 0
    i0 = first // tm
    return pl.pallas_call(
        _rmsnorm_body,
        out_shape=jax.ShapeDtypeStruct((m, d), F32),
        grid=(m // tm,),
        in_specs=[pl.BlockSpec((tm, d), lambda i: (i + i0, 0)), pl.BlockSpec((1, d), lambda i: (0, 0))],
        out_specs=pl.BlockSpec((tm, d), lambda i: (i, 0)),
        compiler_params=_params("parallel"),
        name="final_rmsnorm",
    )(x, g.reshape(1, d))


def _group_norm_gate(o, g):
    mu = jnp.mean(o, axis=-1, keepdims=True)
    d = o - mu
    var = jnp.mean(d * d, axis=-1, keepdims=True)
    return d * lax.rsqrt(var + GN_EPS) * _silu(g)


def _ret_prompt_body(q_ref, k_ref, v_ref, g_ref, dmat_ref, cdec_ref, kdec_ref, gl_ref, o_ref, sfin_ref, s_scr):
    c = pl.program_id(1)

    @pl.when(c == 0)
    def _():
        s_scr[...] = jnp.zeros_like(s_scr)

    L = dmat_ref.shape[1]
    s = s_scr[...]
    for u in range(q_ref.shape[0] // L):
        rows = slice(u * L, (u + 1) * L)
        q = q_ref[rows, :].astype(BF16)
        k = k_ref[rows, :]
        v = v_ref[rows, :]
        a = _dot_nt(q, k.astype(BF16)) * dmat_ref[0]
        inner = _dot(a.astype(BF16), v)
        cross = _dot(q, s.astype(BF16)) * cdec_ref[0]
        kd = (k * kdec_ref[0]).astype(BF16)
        s = gl_ref[0] * s + _dot_tn(kd, v)
        o_ref[rows, :] = _group_norm_gate(inner + cross, g_ref[rows, :].astype(F32)).astype(o_ref.dtype)
    s_scr[...] = s

    @pl.when(c == pl.num_programs(1) - 1)
    def _():
        sfin_ref[0] = s


def _retention_prompt(q, k, vg, s_len, n_heads, dk, dv, log_g):
    L = RET_CHUNK
    per_step = 4 if s_len % (4 * L) == 0 else 1
    T = per_step * L
    i = jnp.arange(L, dtype=F32)
    diff = i[:, None] - i[None, :]
    dmat = jnp.where(diff >= 0, jnp.exp(log_g[:, None, None] * jnp.maximum(diff, 0.0)), 0.0)
    cdec = jnp.exp(log_g[:, None] * (i[None, :] + 1.0))[:, :, None]
    kdec = jnp.exp(log_g[:, None] * (L - 1.0 - i)[None, :])[:, :, None]
    gl = jnp.broadcast_to(jnp.exp(log_g * L)[:, None, None], (n_heads, 1, dv))
    return pl.pallas_call(
        _ret_prompt_body,
        out_shape=(jax.ShapeDtypeStruct((s_len, n_heads * dv), BF16),
                   jax.ShapeDtypeStruct((n_heads, dk, dv), F32)),
        grid=(n_heads, s_len // T),
        in_specs=[pl.BlockSpec((T, dk), lambda h, c: (c, h)),
                  pl.BlockSpec((T, dk), lambda h, c: (c, h)),
                  pl.BlockSpec((T, dv), lambda h, c: (c, h)),
                  pl.BlockSpec((T, dv), lambda h, c: (c, h + n_heads)),
                  pl.BlockSpec((1, L, L), lambda h, c: (h, 0, 0)),
                  pl.BlockSpec((1, L, 1), lambda h, c: (h, 0, 0)),
                  pl.BlockSpec((1, L, 1), lambda h, c: (h, 0, 0)),
                  pl.BlockSpec((1, 1, dv), lambda h, c: (h, 0, 0))],
        out_specs=(pl.BlockSpec((T, dv), lambda h, c: (c, h)),
                   pl.BlockSpec((1, dk, dv), lambda h, c: (h, 0, 0))),
        scratch_shapes=[pltpu.VMEM((dk, dv), F32)],
        compiler_params=_params("parallel", "arbitrary"),
        name="retention_prompt",
    )(q, k, vg, vg, dmat, cdec, kdec, gl)


def _ret_sample_body(n_heads, qc_ref, kc_ref, v_ref, g_ref, g1_ref, s_ref, o_ref, snew_ref):
    qc = qc_ref[0]
    kc = kc_ref[0]
    for h in range(n_heads):
        s = s_ref[0, h]
        qh = qc[:, h:h + 1]
        kh = kc[:, h:h + 1]
        vh = v_ref[0, h:h + 1, :]
        g1 = g1_ref[h:h + 1, :]
        a = jnp.sum(qh * kh, axis=0, keepdims=True)
        cross = jnp.sum(s * qh, axis=0, keepdims=True) * g1
        snew_ref[0, h] = g1 * s + kh * vh
        o = a * vh + cross
        o_ref[0, h:h + 1, :] = _group_norm_gate(o, g_ref[0, h:h + 1, :])


def _retention_sample(q, k, v, g, state, log_g):
    b, n_heads, dk, dv = state.shape
    qc = q.reshape(b, n_heads, dk).transpose(0, 2, 1)
    kc = k.reshape(b, n_heads, dk).transpose(0, 2, 1)
    g1 = jnp.broadcast_to(jnp.exp(log_g * 1.0)[:, None], (n_heads, dv))
    return pl.pallas_call(
        functools.partial(_ret_sample_body, n_heads),
        out_shape=(jax.ShapeDtypeStruct((b, n_heads, dv), F32),
                   jax.ShapeDtypeStruct((b, n_heads, dk, dv), F32)),
        grid=(b,),
        in_specs=[pl.BlockSpec((1, dk, n_heads), lambda i: (i, 0, 0)),
                  pl.BlockSpec((1, dk, n_heads), lambda i: (i, 0, 0)),
                  pl.BlockSpec((1, n_heads, dv), lambda i: (i, 0, 0)),
                  pl.BlockSpec((1, n_heads, dv), lambda i: (i, 0, 0)),
                  pl.BlockSpec((n_heads, dv), lambda i: (0, 0)),
                  pl.BlockSpec((1, n_heads, dk, dv), lambda i: (i, 0, 0, 0))],
        out_specs=(pl.BlockSpec((1, n_heads, dv), lambda i: (i, 0, 0)),
                   pl.BlockSpec((1, n_heads, dk, dv), lambda i: (i, 0, 0, 0))),
        compiler_params=_params("parallel"),
        name="retention_sample",
    )(qc, kc, v.reshape(b, n_heads, dv), g.reshape(b, n_heads, dv), g1, state)


def _select_topk(s, idx, n_valid, axis):
    big = float(s.shape[axis])
    s = jnp.where(idx < n_valid, s, NEG_INF)
    sel = jnp.zeros(s.shape, F32)
    for j in range(min(MOBA_TOPK, s.shape[axis])):
        m = jnp.max(s, axis=axis, keepdims=True)
        first = jnp.min(jnp.where(s == m, idx, big), axis=axis, keepdims=True)
        pick = idx == first
        sel = jnp.where(pick, jnp.where(n_valid > j, 1.0, 0.0), sel)
        s = jnp.where(pick, -jnp.inf, s)
    return sel


def _moba_prompt_body(nb, scale, q_ref, k_ref, v_ref, kmean_ref, o_ref, kmean_scr, vt_scr, sel_scr, m_scr, l_scr,
                      acc_scr, s_buf, p_buf, a_buf):
    blk = MOBA_BLOCK
    sub = blk // 8
    hd = acc_scr.shape[0]
    qb = pl.program_id(1)

    @pl.when(qb == 0)
    def _():
        for n in range(nb):
            kmean_scr[n:n + 1, :] = kmean_ref[n]
            vt_scr[n] = v_ref[n * blk:(n + 1) * blk, :].astype(F32).T.astype(BF16)

    qt = q_ref[...].T
    qt_hi, qt_lo = _split_hi_lo(qt)
    qt_scaled = (qt * (scale * LOG2_E)).astype(BF16)
    km_hi, km_lo = _split_hi_lo(kmean_scr[...])
    s_blk = _dot(km_hi, qt_lo) + _dot(km_lo, qt_hi) + _dot(km_hi, qt_hi)
    bidx = lax.broadcasted_iota(jnp.int32, (nb, blk), 0).astype(F32)
    sel = _select_topk(s_blk, bidx, qb.astype(F32), axis=0)
    for n in range(nb):
        sel_scr[n] = jnp.broadcast_to(sel[n:n + 1, :], (8, blk))

    m_scr[...] = jnp.full(m_scr.shape, NEG_INF, F32)
    l_scr[...] = jnp.zeros_like(l_scr)
    acc_scr[...] = jnp.zeros_like(acc_scr)

    def scores(n):
        kb = k_ref[pl.ds(pl.multiple_of(n * blk, blk), blk), :]
        return _dot(kb, qt_scaled).reshape(sub, 8, blk)

    def softmax_update(ss):
        m_old = m_scr[...]
        mx = functools.reduce(jnp.maximum, [jnp.max(s, axis=0) for s in ss])
        m_new = jnp.maximum(m_old, jnp.max(mx, axis=0, keepdims=True))
        alpha = jnp.exp2(m_old - m_new)
        ps = [jnp.exp2(s - m_new[None]) for s in ss]
        psum = functools.reduce(jnp.add, [jnp.sum(p, axis=0) for p in ps])
        l_scr[...] = alpha * l_scr[...] + jnp.sum(psum, axis=0, keepdims=True)
        m_scr[...] = m_new
        return alpha, [p.reshape(blk, blk).astype(BF16) for p in ps]

    def accumulate(alpha, blocks, pbs):
        pv = functools.reduce(jnp.add, [_dot(vt_scr[n], pb) for n, pb in zip(blocks, pbs)])
        acc_scr[...] = (acc_scr[...].reshape(hd // 8, 8, blk) * alpha[None]).reshape(hd, blk) + pv

    kpos = (lax.broadcasted_iota(jnp.int32, (sub, 8, blk), 0) * 8
            + lax.broadcasted_iota(jnp.int32, (sub, 8, blk), 1))
    qpos = lax.broadcasted_iota(jnp.int32, (sub, 8, blk), 2)
    alpha, pbs = softmax_update([jnp.where(kpos <= qpos, scores(qb), NEG_INF)])
    accumulate(alpha, [qb], pbs)

    n_items = (qb + 1) // 2
    s_buf[1] = jnp.full(s_buf.shape[1:], NEG_INF, F32)
    p_buf[0] = jnp.zeros(p_buf.shape[1:], BF16)
    a_buf[0] = jnp.ones(a_buf.shape[1:], F32)

    def step(t, slot):
        prev = 1 - slot
        b1 = jnp.minimum(2 * t, nb - 2)
        for j in range(2):
            row = jnp.where(t < n_items, sel_scr[b1 + j], 0.0)
            s_buf[slot, j] = jnp.where(row[None] > 0.5, scores(b1 + j), NEG_INF)
        alpha, pbs = softmax_update([s_buf[prev, 0], s_buf[prev, 1]])
        a_buf[prev] = alpha
        p_buf[prev, 0] = pbs[0]
        p_buf[prev, 1] = pbs[1]
        b3 = jnp.clip(2 * (t - 2), 0, nb - 2)
        accumulate(a_buf[slot], [b3, b3 + 1], [p_buf[slot, 0], p_buf[slot, 1]])

    def two_steps(u, carry):
        step(2 * u, 0)
        step(2 * u + 1, 1)
        return carry

    lax.fori_loop(0, jnp.where(n_items > 0, (n_items + 3) // 2, 0), two_steps, 0)

    inv_l = 1.0 / l_scr[...]
    o = (acc_scr[...].reshape(hd // 8, 8, blk) * inv_l[None]).reshape(hd, blk)
    o_ref[...] = o.T.astype(o_ref.dtype)


def _moba_prompt(q, k, v, kmean, s_len, n_heads, hd):
    assert s_len % MOBA_BLOCK == 0 and s_len >= 2 * MOBA_BLOCK
    nb = s_len // MOBA_BLOCK
    blk = MOBA_BLOCK
    return pl.pallas_call(
        functools.partial(_moba_prompt_body, nb, hd ** -0.5),
        out_shape=jax.ShapeDtypeStruct((s_len, n_heads * hd), BF16),
        grid=(n_heads, nb),
        in_specs=[pl.BlockSpec((blk, hd), lambda h, i: (i, h)),
                  pl.BlockSpec((None, s_len, hd), lambda h, i: (h, 0, 0)),
                  pl.BlockSpec((None, s_len, hd), lambda h, i: (h, 0, 0)),
                  pl.BlockSpec((nb, 1, hd), lambda h, i: (0, 0, h))],
        out_specs=pl.BlockSpec((blk, hd), lambda h, i: (i, h)),
        scratch_shapes=[pltpu.VMEM((nb, hd), F32),
                        pltpu.VMEM((nb, hd, blk), BF16),
                        pltpu.VMEM((nb, 8, blk), F32),
                        pltpu.VMEM((8, blk), F32),
                        pltpu.VMEM((8, blk), F32),
                        pltpu.VMEM((hd, blk), F32),
                        pltpu.VMEM((2, 2, blk // 8, 8, blk), F32),
                        pltpu.VMEM((2, 2, blk, blk), BF16),
                        pltpu.VMEM((2, 8, blk), F32)],
        compiler_params=_params("parallel", "arbitrary"),
        name="moba_prompt",
    )(q, k, v, kmean)


def _class_reduce(x, op, period):
    w = x.shape[1]
    y = functools.reduce(op, [x[:, i * LANES:(i + 1) * LANES] for i in range(w // LANES)])
    y = jnp.broadcast_to(y, (8, LANES))
    shift = LANES // 2
    while shift >= period:
        y = op(y, pltpu.roll(y, shift, 1))
        shift //= 2
    return jnp.concatenate([y[0:1]] * (w // LANES), axis=1)


def _block_rows(s, n_blocks, period):
    per_tile = LANES // period
    rows = []
    for n in range(n_blocks):
        tile = jnp.broadcast_to(s[:, (n // per_tile) * LANES:(n // per_tile + 1) * LANES], (8, LANES))
        shift = (LANES - (n % per_tile) * period) % LANES
        rows.append((pltpu.roll(tile, shift, 1) if shift else tile)[0:1])
    return jnp.concatenate(rows, axis=0)


def _replicate_classes(x, period):
    lane = lax.broadcasted_iota(jnp.int32, x.shape, 1)
    x = jnp.where(lane < period, x, 0.0)
    shift = period
    while shift < LANES:
        x = x + pltpu.roll(x, shift, 1)
        shift *= 2
    return x


def _moba_sample_body(G, NG, page, scale, pt_ref, *refs):
    ck = refs[:G]
    cv = refs[G:2 * G]
    q_ref, kn_ref, vn_ref, diag_ref, o_ref, s_all, psum, p_all, acc, pnew = refs[2 * G:]
    H, hd = q_ref.shape[1:]
    W = page * H
    NP = G * NG
    ppb = MOBA_BLOCK // page
    cb = NP // ppb
    t = pl.program_id(1)

    def diag_rows(res):
        return jnp.sum(res * diag_ref[:, :res.shape[1]], axis=0, keepdims=True)

    def pad_rows(x, rows):
        return jnp.concatenate([x, jnp.zeros((rows - x.shape[0], x.shape[1]), x.dtype)], axis=0)

    @pl.when(t < NG)
    def _k_phase():
        q_bf = q_ref[0].astype(BF16)
        for g in range(G):
            kp = ck[g][0]
            psum[t * G + g] = jnp.sum(kp.reshape(page, H, hd), axis=0)
            s_all[t, g:g + 1, :] = diag_rows(_dot_nt(q_bf, kp.astype(BF16))) * scale

    @pl.when(t == NG - 1)
    def _select_and_softmax():
        q_hi, q_lo = _split_hi_lo(q_ref[0])
        kn = kn_ref[0]
        kmean = [sum(psum[n * ppb + j] for j in range(ppb)) / MOBA_BLOCK for n in range(cb)] + [kn / MOBA_BLOCK]
        wb = -(-(cb + 1) * H // LANES) * LANES
        km_hi, km_lo = _split_hi_lo(pad_rows(jnp.concatenate(kmean, axis=0), wb))
        s_blk = diag_rows(_dot_nt(q_lo, km_hi) + _dot_nt(q_hi, km_lo) + _dot_nt(q_hi, km_hi))
        s_rows = pad_rows(_block_rows(s_blk, cb + 1, H), -(-(cb + 1) // 8) * 8)
        bidx = lax.broadcasted_iota(jnp.int32, s_rows.shape, 0).astype(F32)
        sel = _replicate_classes(_select_topk(s_rows, bidx, float(cb), axis=0), H)
        s_new = _class_reduce(diag_rows(_dot_nt(q_hi, pad_rows(kn, LANES).astype(BF16))), jnp.add, H) * scale
        s_new = jnp.concatenate([s_new] * (W // LANES), axis=1)
        m = None
        for pg in range(NP):
            chosen = jnp.concatenate([sel[pg // ppb:pg // ppb + 1, :]] * (W // LANES), axis=1)
            sm = jnp.where(chosen > 0.5, s_all[pg // G, pg % G:pg % G + 1, :], NEG_INF)
            s_all[pg // G, pg % G:pg % G + 1, :] = sm
            m = sm if m is None else jnp.maximum(m, sm)
        m = jnp.maximum(_class_reduce(m, jnp.maximum, H), s_new)
        l = None
        for pg in range(NP):
            p = jnp.exp(s_all[pg // G, pg % G:pg % G + 1, :] - m)
            s_all[pg // G, pg % G:pg % G + 1, :] = p
            l = p if l is None else l + p
        p_new = jnp.exp(s_new - m)
        inv = 1.0 / (_class_reduce(l, jnp.add, H) + p_new)
        for pg in range(NP):
            p_all[pg // G, pg % G:pg % G + 1, :] = s_all[pg // G, pg % G:pg % G + 1, :] * inv
        pnew[...] = (p_new * inv)[:, :LANES]

    @pl.when(t == NG)
    def _():
        acc[...] = jnp.zeros_like(acc)

    @pl.when(t >= NG)
    def _v_phase():
        for g in range(G):
            pm = (jnp.broadcast_to(p_all[t - NG, g:g + 1, :], (H, W)) * diag_ref[...]).astype(BF16)
            acc[...] += _dot(pm, cv[g][0].astype(BF16))

    @pl.when(t == 2 * NG - 1)
    def _():
        pm = (jnp.broadcast_to(pnew[...], (H, LANES)) * diag_ref[:, :LANES]).astype(BF16)
        o_ref[0] = acc[...] + _dot(pm, pad_rows(vn_ref[0], LANES).astype(BF16))


def _moba_sample(q, kn, vn, cache_k, cache_v, page_table):
    n_pool, page, n_heads, hd = cache_k.shape
    b, n_pages = page_table.shape
    da = n_heads * hd
    assert MOBA_BLOCK % page == 0 and n_pages % (MOBA_BLOCK // page) == 0
    assert n_heads % 8 == 0 and LANES % n_heads == 0 and n_heads & (n_heads - 1) == 0 and hd == LANES
    G = next(g for g in (8, 4, 2, 1) if n_pages % g == 0)
    NG = n_pages // G
    w = page * n_heads
    diag = (jnp.arange(w)[None, :] % n_heads == jnp.arange(n_heads)[:, None]).astype(F32)

    def k_map(g):
        return lambda i, t, pt: (pt[i, jnp.minimum(t, NG - 1) * G + g], 0, 0)

    def v_map(g):
        return lambda i, t, pt: (pt[i, jnp.maximum(t - NG, 0) * G + g], 0, 0)

    ck = cache_k.reshape(n_pool, w, hd)
    cv = cache_v.reshape(n_pool, w, hd)
    per_b = lambda i, t, pt: (i, 0, 0)
    grid_spec = pltpu.PrefetchScalarGridSpec(
        num_scalar_prefetch=1,
        grid=(b, 2 * NG),
        in_specs=[pl.BlockSpec((1, w, hd), k_map(g)) for g in range(G)]
        + [pl.BlockSpec((1, w, hd), v_map(g)) for g in range(G)]
        + [pl.BlockSpec((1, n_heads, hd), per_b)] * 3
        + [pl.BlockSpec((n_heads, w), lambda i, t, pt: (0, 0))],
        out_specs=pl.BlockSpec((1, n_heads, hd), per_b),
        scratch_shapes=[pltpu.VMEM((NG, G, w), F32),
                        pltpu.VMEM((n_pages, n_heads, hd), F32),
                        pltpu.VMEM((NG, G, w), F32),
                        pltpu.VMEM((n_heads, hd), F32),
                        pltpu.VMEM((1, LANES), F32)])
    out = pl.pallas_call(
        functools.partial(_moba_sample_body, G, NG, page, hd ** -0.5),
        out_shape=jax.ShapeDtypeStruct((b, n_heads, hd), F32),
        grid_spec=grid_spec,
        compiler_params=_params("parallel", "arbitrary"),
        name="moba_sample",
    )(page_table, *([ck] * G), *([cv] * G), q.reshape(b, n_heads, hd), kn.reshape(b, n_heads, hd),
      vn.reshape(b, n_heads, hd), diag)
    return out.reshape(b, da)


def _rope_tables(pos, d_head):
    inv = ROPE_THETA ** (-jnp.arange(0, d_head, 2, dtype=F32) / d_head)
    ang = pos.astype(F32)[:, None] * inv[None, :]
    cos, sin = jnp.cos(ang), jnp.sin(ang)
    if d_head // 2 == LANES:
        return cos, sin
    assert d_head == LANES
    return jnp.concatenate([cos, cos], axis=1), jnp.concatenate([-sin, sin], axis=1)


def _ffn(h, g, w_in, w_out, layer):
    a = _norm_matmul(h, g[layer], w_in, BF16, mode="swiglu", layer=layer)
    return _matmul_residual(a, w_out, h, 0.5, layer=layer)


def kernel(x_prompt, x_sample, state_ret, cache_k, cache_v, page_table, norm_ffa, ffa_w_in, ffa_w_out, norm_mix, norm_ffb, ffb_w_in, ffb_w_out, ret_w_in, ret_w_out, kv_norm, w_kv, moba_w_q, moba_w_o, final_norm):
    bp, s_len, d = x_prompt.shape
    db, dl, _ = x_sample.shape
    n_a, _, ret_heads, dk, dv = state_ret.shape
    _, page, att_heads, hd = cache_k.shape
    assert bp == 1 and dl == 1 and n_a == 1 and norm_ffa.shape[0] == 2 and s_len % db == 0
    assert s_len % RET_CHUNK == 0 and s_len % MOBA_BLOCK == 0
    past_len = page_table.shape[1] * page
    qk_w, v_w, att_w = ret_heads * dk, ret_heads * dv, att_heads * hd

    pos = jnp.concatenate([jnp.arange(s_len, dtype=jnp.int32), jnp.full((db,), past_len, jnp.int32)])
    cos_r, sin_r = _rope_tables(pos, dk)
    cos_m, sin_m = _rope_tables(pos, hd)
    log_g = jnp.log(1.0 - 2.0 ** (-5.0 - jnp.arange(ret_heads, dtype=F32)))

    ffa_w_out, ffb_w_out, ret_w_out, moba_w_o = (p.astype(BF16) for p in (ffa_w_out, ffb_w_out, ret_w_out, moba_w_o))
    w_kv3 = w_kv[None]

    h = jnp.concatenate([x_prompt[0], x_sample[:, 0]], axis=0)

    h = _ffn(h, norm_ffa, ffa_w_in, ffa_w_out, 0)
    q = _norm_matmul(h, norm_mix[0], ret_w_in, F32, "rope", cos_r, sin_r, dk, cols=(0, qk_w))
    k = _norm_matmul(h, norm_mix[0], ret_w_in, F32, "rope", cos_r, sin_r, dk, dk ** -0.5, cols=(qk_w, qk_w))
    vg = _norm_matmul(h, norm_mix[0], ret_w_in, BF16, cols=(2 * qk_w, 2 * v_w))
    o_p, st_p = _retention_prompt(q, k, vg, s_len, ret_heads, dk, dv, log_g)
    vg_s = vg[s_len:].astype(F32)
    o_s, st_s = _retention_sample(q[s_len:], k[s_len:], vg_s[:, :v_w], vg_s[:, v_w:], state_ret[0], log_g)
    o = jnp.concatenate([o_p, o_s.reshape(db, v_w).astype(BF16)], axis=0)
    h = _matmul_residual(o, ret_w_out, h, 1.0)
    h = _ffn(h, norm_ffb, ffb_w_in, ffb_w_out, 0)

    k_cols, v_cols = (0, att_w), (att_w, att_w)
    prompt_rows, sample_rows = (0, s_len), (s_len, db)
    k_p, k_hm, k_mean = _norm_matmul(h, kv_norm, w_kv3, F32, "rope", cos_m, sin_m, hd, rows=prompt_rows, cols=k_cols,
                                     aux=True)
    k_s = _norm_matmul(h, kv_norm, w_kv3, F32, "rope", cos_m, sin_m, hd, rows=sample_rows, cols=k_cols)
    v_p, v_hm = _norm_matmul(h, kv_norm, w_kv3, F32, rows=prompt_rows, cols=v_cols, aux=True)
    v_s = _norm_matmul(h, kv_norm, w_kv3, F32, rows=sample_rows, cols=v_cols)

    h = _ffn(h, norm_ffa, ffa_w_in, ffa_w_out, 1)
    q = _norm_matmul(h, norm_mix[1], moba_w_q, F32, "rope", cos_m, sin_m, hd)
    a_p = _moba_prompt(q, k_hm, v_hm, k_mean, s_len, att_heads, hd)
    a_s = _moba_sample(q[s_len:], k_s, v_s, cache_k, cache_v, page_table)
    a = jnp.concatenate([a_p, a_s.astype(BF16)], axis=0)
    h = _matmul_residual(a, moba_w_o, h, 1.0)
    h = _ffn(h, norm_ffb, ffb_w_in, ffb_w_out, 1)
    y_p = _rmsnorm(h, final_norm, prompt_rows)
    y_s = _rmsnorm(h, final_norm, sample_rows)

    return (y_p.reshape(1, s_len, d),
            y_s.reshape(db, 1, d),
            st_p[None, None],
            st_s[None],
            k_p.reshape(1, s_len, att_heads, hd),
            v_p.reshape(1, s_len, att_heads, hd),
            k_s.reshape(db, 1, att_heads, hd),
            v_s.reshape(db, 1, att_heads, hd))
```
